```python
import jax, jax.numpy as jnp
from jax import lax
import numpy as np

D_MODEL = 1024
BATCH = 1
SEQ = 16384
DEPTH = 4

GRID_W = 64
CTX_LEN = 256
A_HEADS = 4
A_DQK = 128
A_DV = 256
A_QK_WIDTH = A_HEADS * A_DQK
A_WIDTH = A_HEADS * A_DV
MLSTM_CHUNK = 128
SHORT_CONV_W = 5
B_GROUPS = 4
B_WIDTH = 1024
B_CHUNK = 128
ROWS_PER_CHUNK = B_CHUNK // GRID_W
D_FF = -(-(8 * D_MODEL) // (3 * 256)) * 256
N_MOD = 6
EPS = 1e-6
IN_SIZES = (A_QK_WIDTH, A_QK_WIDTH, A_WIDTH, A_WIDTH, 4 * A_HEADS, B_WIDTH, B_WIDTH, D_MODEL, D_MODEL)
IN_WIDTH = sum(IN_SIZES)
IN_SPLITS = tuple(int(s) for s in np.cumsum(IN_SIZES)[:-1])

kernel_name = 'hybrid_mlstm_gmlp_diffusion_block'


def _rmsnorm(x, g):
    xf = x.astype(jnp.float32)
    y = xf * lax.rsqrt(jnp.mean(xf * xf, axis=-1, keepdims=True) + EPS)
    return (y * g.astype(jnp.float32)).astype(x.dtype)


def _modulate(h, shift, scale):
    return h * (1 + scale[:, None, :]) + shift[:, None, :]


def _short_conv(x, w, b):
    pad = SHORT_CONV_W // 2
    y = lax.conv_general_dilated(x, w[:, None, :].astype(x.dtype), window_strides=(1,), padding=[(pad, pad)], dimension_numbers=('NWC', 'WIO', 'NWC'), feature_group_count=x.shape[-1])
    return y + b


def _heads(t, d):
    B, N, _ = t.shape
    return t.reshape(B, N, -1, d).transpose(0, 2, 1, 3)


def _mlstm_chunkwise(q, k, v, log_i, log_f, state):
    q, k, v = (t.astype(jnp.float32) for t in (q, k, v))
    B, H, N, _ = q.shape
    L = MLSTM_CHUNK
    nc = N // L

    def to_chunks(t):
        return jnp.moveaxis(t.reshape(B, H, nc, L, *t.shape[3:]), 2, 0)

    xs = tuple(to_chunks(t) for t in (q, k, v, log_i, log_f))
    lower = jnp.tril(jnp.ones((L, L), dtype=bool))

    def step(carry, inp):
        C, n, m = carry
        qc, kc, vc, ic, fc = inp
        b = jnp.cumsum(fc, axis=-1)
        log_d = b[..., :, None] - b[..., None, :] + ic[..., None, :]
        log_d = jnp.where(lower, log_d, -jnp.inf)
        log_prev = b + m[..., None]
        m_t = jnp.maximum(log_prev, jnp.max(log_d, axis=-1))
        w_prev = jnp.exp(log_prev - m_t)
        s = jnp.einsum('bhtd,bhsd->bhts', qc, kc) * jnp.exp(log_d - m_t[..., None])
        num = jnp.einsum('bhts,bhsv->bhtv', s, vc) + w_prev[..., None] * jnp.einsum('bhtd,bhdv->bhtv', qc, C)
        den = jnp.sum(s, axis=-1) + w_prev * jnp.einsum('bhtd,bhd->bht', qc, n)
        h = num / jnp.maximum(jnp.abs(den), jnp.exp(-m_t))[..., None]
        b_end = b[..., -1]
        log_src = b_end[..., None] - b + ic
        m_new = jnp.maximum(b_end + m, jnp.max(log_src, axis=-1))
        w_src = jnp.exp(log_src - m_new[..., None])
        w_c = jnp.exp(b_end + m - m_new)
        kw = kc * w_src[..., None]
        C_new = w_c[..., None, None] * C + jnp.einsum('bhsd,bhsv->bhdv', kw, vc)
        n_new = w_c[..., None] * n + jnp.sum(kw, axis=2)
        return (C_new, n_new, m_new), h

    state, hs = lax.scan(step, state, xs)
    h = jnp.moveaxis(hs, 0, 2).reshape(B, H, N, v.shape[-1])
    return h, state


def _mlstm_direction(ctx_in, lat_in, reverse):
    if reverse:
        ctx_in = tuple(jnp.flip(t, axis=2) for t in ctx_in)
        lat_in = tuple(jnp.flip(t, axis=2) for t in lat_in)
    B = ctx_in[0].shape[0]
    state0 = (jnp.zeros((B, A_HEADS, A_DQK, A_DV), jnp.float32), jnp.zeros((B, A_HEADS, A_DQK), jnp.float32), jnp.zeros((B, A_HEADS), jnp.float32))
    h_c, state = _mlstm_chunkwise(*ctx_in, state0)
    h_x, _ = _mlstm_chunkwise(*lat_in, state)
    if reverse:
        h_c = jnp.flip(h_c, axis=2)
        h_x = jnp.flip(h_x, axis=2)
    return h_c, h_x


def _mix_inputs(h, w_in, conv_w, conv_b, gate_b):
    B, N, _ = h.shape
    z = jnp.einsum('bnd,de->bne', h, w_in)
    q_raw, k_raw, v_raw, o_raw, g_raw, u_raw, vb_raw, ga_raw, gb_raw = jnp.split(z, IN_SPLITS, axis=-1)
    qk = jax.nn.silu(_short_conv(jnp.concatenate([q_raw, k_raw], axis=-1), conv_w, conv_b))
    q, k = jnp.split(qk, 2, axis=-1)
    q = _heads(q, A_DQK) * (A_DQK ** -0.5)
    k = _heads(k, A_DQK)
    v = _heads(v_raw, A_DV)
    g = (g_raw + gate_b).astype(jnp.float32).reshape(B, N, 4, A_HEADS).transpose(2, 0, 3, 1)
    fwd = (q, k, v, g[0], jax.nn.log_sigmoid(g[1]))
    bwd = (q, k, v, g[2], jax.nn.log_sigmoid(g[3]))
    rest = (o_raw, u_raw, vb_raw, ga_raw, gb_raw)
    return fwd, bwd, rest


def _mix_outputs(h_a, rest, n_chunks, head_g, vnorm_g, w_sp, b_sp, w_pa, w_pb, w_out):
    o_raw, u_raw, vb_raw, ga_raw, gb_raw = rest
    B, N, _ = o_raw.shape
    y_a = _rmsnorm(h_a.astype(o_raw.dtype), head_g[:, None, :]).transpose(0, 2, 1, 3).reshape(B, N, A_WIDTH)
    y_a = y_a * jax.nn.sigmoid(o_raw)
    u = jax.nn.gelu(u_raw)
    vb = _rmsnorm(jax.nn.gelu(vb_raw), vnorm_g).reshape(B, n_chunks, B_CHUNK, B_GROUPS, B_WIDTH // B_GROUPS)
    sv = jnp.einsum('gts,bcsgd->bctgd', w_sp, vb) + b_sp.T[:, :, None]
    y_b = u * sv.reshape(B, N, B_WIDTH)
    merged = jax.nn.sigmoid(ga_raw) * jnp.einsum('bne,ed->bnd', y_a, w_pa) + jax.nn.sigmoid(gb_raw) * jnp.einsum('bne,ed->bnd', y_b, w_pb)
    return jnp.einsum('bnd,de->bne', merged, w_out)


def _swiglu(h, w1, w3, w2):
    a = jnp.einsum('bnd,df->bnf', h, w1)
    g = jnp.einsum('bnd,df->bnf', h, w3)
    return jnp.einsum('bnf,fd->bnd', jax.nn.silu(a) * g, w2)


def setup_inputs(seed: int = 0) -> dict:
    key = jax.random.key(seed)
    ks = jax.random.split(key, 24)
    L, D = DEPTH, D_MODEL

    def nrm(k, shape, scale):
        return jax.random.normal(k, shape, jnp.float32) * scale

    gate_i = nrm(ks[11], (L, 2, A_HEADS), 0.1)
    gate_f = jnp.linspace(3.0, 6.0, A_HEADS, dtype=jnp.float32) + nrm(ks[12], (L, 2, A_HEADS), 0.1)
    gate_b = jnp.stack([gate_i, gate_f], axis=2).reshape(L, 4 * A_HEADS)
    return {
        'x': nrm(ks[0], (BATCH, SEQ, D), 1.0),
        'c': nrm(ks[1], (BATCH, D), 1.0),
        'ctx': nrm(ks[2], (BATCH, CTX_LEN, D), 1.0),
        'c_ctx': nrm(ks[3], (D,), 1.0),
        'w_ada': nrm(ks[4], (L, D, N_MOD * D), 0.5 * D ** -0.5),
        'b_ada': nrm(ks[5], (L, N_MOD * D), 0.02),
        'norm1_g': 1.0 + nrm(ks[6], (L, D), 0.1),
        'norm2_g': 1.0 + nrm(ks[7], (L, D), 0.1),
        'w_in': nrm(ks[8], (L, D, IN_WIDTH), D ** -0.5),
        'conv_w': nrm(ks[9], (L, SHORT_CONV_W, 2 * A_QK_WIDTH), SHORT_CONV_W ** -0.5),
        'conv_b': nrm(ks[10], (L, 2 * A_QK_WIDTH), 0.02),
        'gate_b': gate_b,
        'head_g': 1.0 + nrm(ks[13], (L, A_HEADS, A_DV), 0.1),
        'vnorm_g': 1.0 + nrm(ks[14], (L, B_WIDTH), 0.1),
        'w_spatial': nrm(ks[15], (L, B_GROUPS, B_CHUNK, B_CHUNK), 0.5 * B_CHUNK ** -0.5),
        'b_spatial': 1.0 + nrm(ks[16], (L, B_GROUPS, B_CHUNK), 0.1),
        'w_pa': nrm(ks[17], (L, A_WIDTH, D), A_WIDTH ** -0.5),
        'w_pb': nrm(ks[18], (L, B_WIDTH, D), B_WIDTH ** -0.5),
        'w_out': nrm(ks[19], (L, D, D), D ** -0.5),
        'w_ff1': nrm(ks[20], (L, D, D_FF), D ** -0.5),
        'w_ff3': nrm(ks[21], (L, D, D_FF), D ** -0.5),
        'w_ff2': nrm(ks[22], (L, D_FF, D), D_FF ** -0.5),
        'final_g': 1.0 + nrm(ks[23], (D,), 0.1),
    }


def reference(x, c, ctx, c_ctx, w_ada, b_ada, norm1_g, norm2_g, w_in, conv_w, conv_b, gate_b, head_g, vnorm_g, w_spatial, b_spatial, w_pa, w_pb, w_out, w_ff1, w_ff3, w_ff2, final_g):
    rows = x.shape[1] // GRID_W
    lat_chunks = rows // ROWS_PER_CHUNK
    ctx_chunks = ctx.shape[1] // B_CHUNK
    mod_x = jnp.einsum('bd,lde->lbe', jax.nn.silu(c), w_ada) + b_ada[:, None, :]
    mod_c = jnp.einsum('d,lde->le', jax.nn.silu(c_ctx), w_ada)[:, None, :] + b_ada[:, None, :]
    xs, cs = x, ctx
    for l in range(DEPTH):
        last = l == DEPTH - 1
        sh1x, sc1x, g1x, sh2x, sc2x, g2x = jnp.split(mod_x[l], N_MOD, axis=-1)
        sh1c, sc1c, g1c, sh2c, sc2c, g2c = jnp.split(mod_c[l], N_MOD, axis=-1)
        hx = _modulate(_rmsnorm(xs, norm1_g[l]), sh1x, sc1x)
        hc = _modulate(_rmsnorm(cs, norm1_g[l]), sh1c, sc1c)
        fwd_x, bwd_x, rest_x = _mix_inputs(hx, w_in[l], conv_w[l], conv_b[l], gate_b[l])
        fwd_c, bwd_c, rest_c = _mix_inputs(hc, w_in[l], conv_w[l], conv_b[l], gate_b[l])
        hc_f, hx_f = _mlstm_direction(fwd_c, fwd_x, reverse=False)
        hc_b, hx_b = _mlstm_direction(bwd_c, bwd_x, reverse=True)
        out_x = _mix_outputs(hx_f + hx_b, rest_x, lat_chunks, head_g[l], vnorm_g[l], w_spatial[l], b_spatial[l], w_pa[l], w_pb[l], w_out[l])
        xs = xs + g1x[:, None, :] * out_x
        xs = xs + g2x[:, None, :] * _swiglu(_modulate(_rmsnorm(xs, norm2_g[l]), sh2x, sc2x), w_ff1[l], w_ff3[l], w_ff2[l])
        if not last:
            out_c = _mix_outputs(hc_f + hc_b, rest_c, ctx_chunks, head_g[l], vnorm_g[l], w_spatial[l], b_spatial[l], w_pa[l], w_pb[l], w_out[l])
            cs = cs + g1c[:, None, :] * out_c
            cs = cs + g2c[:, None, :] * _swiglu(_modulate(_rmsnorm(cs, norm2_g[l]), sh2c, sc2c), w_ff1[l], w_ff3[l], w_ff2[l])
    return _rmsnorm(xs, final_g)
```

```python
import functools

import jax
import jax.numpy as jnp
from jax import lax
from jax.experimental import pallas as pl
from jax.experimental.pallas import tpu as pltpu

F32 = jnp.float32
BF16 = jnp.bfloat16
HIGHEST = lax.Precision.HIGHEST

EPS = 1e-6
HEADS = 4
DQK = 128
DV = 256
GROUPS = 4
CHUNK = 128
CONV_W = 5
N_MOD = 6
HALO = 8
TILE = 2 * CHUNK
LANES = 128
VMEM_LIMIT = 56 * 1024 * 1024


def _silu(x):
    return x * jax.nn.sigmoid(x)


def _gelu_tanh(x):
    return 0.5 * x * (1.0 + jnp.tanh(0.7978845608028654 * (x + 0.044715 * (x * x * x))))


def _rms(x, g):
    return x * lax.rsqrt(jnp.mean(x * x, axis=-1, keepdims=True) + EPS) * g


def _dot(a, b):
    return jnp.dot(a, b, preferred_element_type=F32)


def _const_spec(shape):
    zeros = (0,) * len(shape)
    return pl.BlockSpec(shape, lambda *_: zeros, pipeline_mode=pl.Buffered(1))


def _ada_kernel(cc_ref, w_ref, b_ref, o_ref):
    s = _silu(cc_ref[...])
    o_ref[0] = jnp.dot(s, w_ref[0], preferred_element_type=F32, precision=HIGHEST) + b_ref[0]


def _ada_call(cc, w_ada, b_ada):
    depth, d, width = w_ada.shape
    tn = width // 4
    return pl.pallas_call(
        _ada_kernel,
        grid=(depth, width // tn),
        in_specs=[
            pl.BlockSpec((8, d), lambda l, j: (0, 0)),
            pl.BlockSpec((1, d, tn), lambda l, j: (l, 0, j)),
            pl.BlockSpec((1, 1, tn), lambda l, j: (l, 0, j)),
        ],
        out_specs=pl.BlockSpec((1, 8, tn), lambda l, j: (l, 0, j)),
        out_shape=jax.ShapeDtypeStruct((depth, 8, width), F32),
        compiler_params=pltpu.CompilerParams(
            dimension_semantics=("arbitrary", "arbitrary"), vmem_limit_bytes=VMEM_LIMIT),
        name="ada_mod",
    )(cc, w_ada, b_ada.reshape(depth, 1, width))


def _in_proj_kernel(x_ref, xp_ref, xn_ref, mod_ref, g_ref, wm_ref, wgi_ref, wgf_ref, cw_ref, cb_ref,
                    gbi_ref, gbf_ref, q_ref, k_ref, v_ref, rest_ref, col_ref, row_ref, zs_ref,
                    *, n_ctx_tiles, n_tiles):
    d = x_ref.shape[1]
    t = x_ref.shape[0]
    qk_w = q_ref.shape[1] + k_ref.shape[1]
    i = pl.program_id(0)
    mod = jnp.where(i < n_ctx_tiles, mod_ref[1:2, :], mod_ref[0:1, :])
    shift = mod[:, 0:d]
    scale = mod[:, d:2 * d]
    g = g_ref[...]

    def norm_mod(x):
        return _rms(x, g) * (1.0 + scale) + shift

    hb = norm_mod(x_ref[...]).astype(BF16)
    hh = norm_mod(jnp.concatenate([xp_ref[...], xn_ref[...]], axis=0)).astype(BF16)

    w_qk = wm_ref[:, 0:qk_w]
    zh = _dot(hh, w_qk)
    prev_ok = jnp.logical_and(i != 0, i != n_ctx_tiles)
    next_ok = jnp.logical_and(i != n_ctx_tiles - 1, i != n_tiles - 1)
    zs_ref[0:HALO, :] = jnp.where(prev_ok, zh[0:HALO], 0.0)
    zs_ref[HALO:HALO + t, :] = _dot(hb, w_qk)
    zs_ref[HALO + t:2 * HALO + t, :] = jnp.where(next_ok, zh[HALO:2 * HALO], 0.0)
    acc = cb_ref[...]
    for j in range(CONV_W):
        acc = acc + cw_ref[j:j + 1, :] * zs_ref[pl.ds(HALO - CONV_W // 2 + j, t), :]
    qk = _silu(acc)
    q_ref[...] = (qk[:, 0:q_ref.shape[1]] * (DQK ** -0.5)).astype(BF16)
    k_ref[...] = qk[:, q_ref.shape[1]:].astype(BF16)

    v_w = v_ref.shape[1]
    v_ref[...] = _dot(hb, wm_ref[:, qk_w:qk_w + v_w]).astype(BF16)
    off = qk_w + v_w
    for c0 in range(0, rest_ref.shape[1], d):
        rest_ref[:, c0:c0 + d] = _dot(hb, wm_ref[:, off + c0:off + c0 + d])

    gi = _dot(hb, wgi_ref[...]) + gbi_ref[...]
    gf = _dot(hb, wgf_ref[...]) + gbf_ref[...]
    logf = jnp.minimum(gf, 0.0) - jnp.log1p(jnp.exp(-jnp.abs(gf)))
    ti = lax.broadcasted_iota(jnp.int32, (CHUNK, CHUNK), 0)
    si = lax.broadcasted_iota(jnp.int32, (CHUNK, CHUNK), 1)
    lower = si <= ti
    upper = si >= ti
    for c in range(t // CHUNK):
        rows = slice(c * CHUNK, (c + 1) * CHUNK)
        lf = logf[rows]
        prefix = jnp.dot(lower.astype(F32), lf, preferred_element_type=F32, precision=HIGHEST)
        suffix = jnp.dot(upper.astype(F32), lf, preferred_element_type=F32, precision=HIGHEST)
        b = jnp.where(si < HEADS, prefix, suffix)
        a = gi[rows] - b
        a_t = a.T
        a_max = jnp.zeros((CHUNK, LANES), F32)
        for hd in range(2 * HEADS):
            msk = lower if hd < HEADS else upper
            run = jnp.max(jnp.where(msk, a_t[hd:hd + 1, :], -jnp.inf), axis=-1, keepdims=True)
            a_max = jnp.where(si == hd, run, a_max)
        col_ref[rows, 0:LANES] = a
        col_ref[rows, LANES:2 * LANES] = a_max
        col_ref[rows, 2 * LANES:3 * LANES] = b
        row_ref[:, rows] = a_t[0:2 * HEADS, :]


def _in_proj_call(xs, mod, g1, wm, wgi, wgf, cw, cb, gbi, gbf, *, n_ctx_tiles):
    n, d = xs.shape
    t = TILE
    n_tiles = n // t
    hb = t // HALO
    last_halo = n // HALO - 1
    qk_w = 2 * HEADS * DQK
    v_w = HEADS * DV
    rest_w = wm.shape[1] - qk_w - v_w
    tile = lambda i: (i, 0)
    kern = functools.partial(_in_proj_kernel, n_ctx_tiles=n_ctx_tiles, n_tiles=n_tiles)
    return pl.pallas_call(
        kern,
        grid=(n_tiles,),
        in_specs=[
            pl.BlockSpec((t, d), tile),
            pl.BlockSpec((HALO, d), lambda i: (jnp.maximum(i * hb - 1, 0), 0)),
            pl.BlockSpec((HALO, d), lambda i: (jnp.minimum((i + 1) * hb, last_halo), 0)),
            _const_spec(mod.shape), _const_spec(g1.shape), _const_spec(wm.shape),
            _const_spec(wgi.shape), _const_spec(wgf.shape), _const_spec(cw.shape),
            _const_spec(cb.shape), _const_spec(gbi.shape), _const_spec(gbf.shape),
        ],
        out_specs=[
            pl.BlockSpec((t, qk_w // 2), tile),
            pl.BlockSpec((t, qk_w // 2), tile),
            pl.BlockSpec((t, v_w), tile),
            pl.BlockSpec((t, rest_w), tile),
            pl.BlockSpec((t, 3 * LANES), tile),
            pl.BlockSpec((2 * HEADS, t), lambda i: (0, i)),
        ],
        out_shape=[
            jax.ShapeDtypeStruct((n, qk_w // 2), BF16),
            jax.ShapeDtypeStruct((n, qk_w // 2), BF16),
            jax.ShapeDtypeStruct((n, v_w), BF16),
            jax.ShapeDtypeStruct((n, rest_w), F32),
            jax.ShapeDtypeStruct((n, 3 * LANES), F32),
            jax.ShapeDtypeStruct((2 * HEADS, n), F32),
        ],
        scratch_shapes=[pltpu.VMEM((t + 2 * HALO, qk_w), F32)],
        compiler_params=pltpu.CompilerParams(
            dimension_semantics=("arbitrary",), vmem_limit_bytes=VMEM_LIMIT),
        name="in_proj",
    )(xs, xs, xs, mod, g1, wm, wgi, wgf, cw, cb, gbi, gbf)


def _mlstm_head(q, k, v, a_col, amax_col, b_col, a_row, c_ref, n_ref, m_ref, hd, reverse):
    ti = lax.broadcasted_iota(jnp.int32, (CHUNK, CHUNK), 0)
    si = lax.broadcasted_iota(jnp.int32, (CHUNK, CHUNK), 1)
    mask = (si >= ti) if reverse else (si <= ti)
    end = 0 if reverse else CHUNK - 1

    m = m_ref[hd:hd + 1, 0:1]
    c_state = c_ref[hd]
    n_row = n_ref[hd:hd + 1, :]

    m_run = jnp.maximum(amax_col, m)
    decay = jnp.where(mask, jnp.exp(a_row - m_run), 0.0)
    s = lax.dot_general(q, k, (((1,), (1,)), ((), ())), preferred_element_type=F32) * decay
    w_prev = jnp.exp(m - m_run)
    num = _dot(s.astype(BF16), v) + w_prev * _dot(q, c_state.astype(BF16))
    qn = jnp.sum(q.astype(F32) * n_row, axis=-1, keepdims=True)
    den = jnp.sum(s, axis=-1, keepdims=True) + w_prev * qn
    floor = jnp.exp(-(b_col + m_run))
    h = num * (1.0 / jnp.maximum(jnp.abs(den), floor))

    m_end = jnp.maximum(amax_col[end:end + 1, :], m)
    w_src = jnp.exp(a_col - m_end)
    w_c = jnp.exp(m - m_end)
    kw = k.astype(F32) * w_src
    c_ref[hd] = w_c * c_state + lax.dot_general(
        kw.astype(BF16), v, (((0,), (0,)), ((), ())), preferred_element_type=F32)
    n_ref[hd:hd + 1, :] = w_c * n_row + jnp.sum(kw, axis=0, keepdims=True)
    m_ref[hd:hd + 1, :] = jnp.broadcast_to(b_col[end:end + 1, :] + m_end, (1, LANES))
    return h


def _mlstm_kernel(qf_ref, kf_ref, vf_ref, colf_ref, rowf_ref, qb_ref, kb_ref, vb_ref, colb_ref, rowb_ref,
                  hf_ref, hb_ref, c_ref, n_ref, m_ref):
    @pl.when(pl.program_id(0) == 0)
    def _():
        c_ref[...] = jnp.zeros_like(c_ref)
        n_ref[...] = jnp.zeros_like(n_ref)
        m_ref[...] = jnp.zeros_like(m_ref)

    for reverse, (q_ref, k_ref, v_ref, col_ref, row_ref, h_ref) in enumerate((
            (qf_ref, kf_ref, vf_ref, colf_ref, rowf_ref, hf_ref),
            (qb_ref, kb_ref, vb_ref, colb_ref, rowb_ref, hb_ref))):
        for head in range(HEADS):
            hd = reverse * HEADS + head
            h_ref[:, head * DV:(head + 1) * DV] = _mlstm_head(
                q_ref[:, head * DQK:(head + 1) * DQK],
                k_ref[:, head * DQK:(head + 1) * DQK],
                v_ref[:, head * DV:(head + 1) * DV],
                col_ref[:, hd:hd + 1],
                col_ref[:, LANES + hd:LANES + hd + 1],
                col_ref[:, 2 * LANES + hd:2 * LANES + hd + 1],
                row_ref[hd:hd + 1, :],
                c_ref, n_ref, m_ref, hd, bool(reverse))


def _mlstm_call(q, k, v, col, row, *, n_ctx_chunks):
    n = q.shape[0]
    n_chunks = n // CHUNK
    fwd = lambda j: (j, 0)
    bwd_idx = lambda j: jnp.where(j < n_ctx_chunks, n_ctx_chunks - 1 - j, n_chunks - 1 + n_ctx_chunks - j)
    bwd = lambda j: (bwd_idx(j), 0)
    specs = lambda tok, rowm: [
        pl.BlockSpec((CHUNK, q.shape[1]), tok),
        pl.BlockSpec((CHUNK, k.shape[1]), tok),
        pl.BlockSpec((CHUNK, v.shape[1]), tok),
        pl.BlockSpec((CHUNK, col.shape[1]), tok),
        pl.BlockSpec((row.shape[0], CHUNK), rowm),
    ]
    return pl.pallas_call(
        _mlstm_kernel,
        grid=(n_chunks,),
        in_specs=specs(fwd, lambda j: (0, j)) + specs(bwd, lambda j: (0, bwd_idx(j))),
        out_specs=[pl.BlockSpec((CHUNK, v.shape[1]), fwd), pl.BlockSpec((CHUNK, v.shape[1]), bwd)],
        out_shape=[jax.ShapeDtypeStruct((n, v.shape[1]), F32)] * 2,
        scratch_shapes=[
            pltpu.VMEM((2 * HEADS, DQK, DV), F32),
            pltpu.VMEM((2 * HEADS, DQK), F32),
            pltpu.VMEM((2 * HEADS, LANES), F32),
        ],
        compiler_params=pltpu.CompilerParams(
            dimension_semantics=("arbitrary",), vmem_limit_bytes=VMEM_LIMIT),
        name="mlstm",
    )(q, k, v, col, row, q, k, v, col, row)


def _mix_out_kernel(x_ref, hf_ref, hb_ref, rest_ref, mod_ref, hg_ref, vg_ref, wsp_ref, bsp_ref,
                    wpa_ref, wpb_ref, wo_ref, o_ref, yb_ref, *, n_ctx_tiles, tile_off):
    d = x_ref.shape[1]
    t = x_ref.shape[0]
    gw = d // GROUPS
    i = pl.program_id(0) + tile_off
    mod = jnp.where(i < n_ctx_tiles, mod_ref[1:2, :], mod_ref[0:1, :])
    gate1 = mod[:, 2 * d:3 * d]

    o_gate = jax.nn.sigmoid(rest_ref[:, 0:d])
    ya = []
    for head in range(HEADS):
        cols = slice(head * DV, (head + 1) * DV)
        h = hf_ref[:, cols] + hb_ref[:, cols]
        ya.append((_rms(h, hg_ref[:, cols]) * o_gate[:, cols]).astype(BF16))
    ya = jnp.concatenate(ya, axis=1)

    u = _gelu_tanh(rest_ref[:, d:2 * d])
    vn = _rms(_gelu_tanh(rest_ref[:, 2 * d:3 * d]), vg_ref[...]).astype(BF16)
    for c in range(t // CHUNK):
        rows = slice(c * CHUNK, (c + 1) * CHUNK)
        for grp in range(GROUPS):
            cols = slice(grp * gw, (grp + 1) * gw)
            sv = _dot(wsp_ref[grp], vn[rows, cols]) + bsp_ref[:, grp:grp + 1]
            yb_ref[rows, cols] = (u[rows, cols] * sv).astype(BF16)

    merged = (jax.nn.sigmoid(rest_ref[:, 3 * d:4 * d]) * _dot(ya, wpa_ref[...])
              + jax.nn.sigmoid(rest_ref[:, 4 * d:5 * d]) * _dot(yb_ref[...], wpb_ref[...]))
    o_ref[...] = x_ref[...] + gate1 * _dot(merged.astype(BF16), wo_ref[...])


def _mix_out_call(xs, hf, hb, rest, mod, hg, vg, wsp, bsp, wpa, wpb, wo, *, n_ctx_tiles, tile_off):
    n, d = xs.shape
    t = TILE
    n_out = n - tile_off * t
    tin = lambda i: (i + tile_off, 0)
    kern = functools.partial(_mix_out_kernel, n_ctx_tiles=n_ctx_tiles, tile_off=tile_off)
    return pl.pallas_call(
        kern,
        grid=(n_out // t,),
        in_specs=[
            pl.BlockSpec((t, d), tin), pl.BlockSpec((t, d), tin), pl.BlockSpec((t, d), tin),
            pl.BlockSpec((t, rest.shape[1]), tin),
            _const_spec(mod.shape), _const_spec(hg.shape), _const_spec(vg.shape),
            _const_spec(wsp.shape), _const_spec(bsp.shape),
            _const_spec(wpa.shape), _const_spec(wpb.shape), _const_spec(wo.shape),
        ],
        out_specs=pl.BlockSpec((t, d), lambda i: (i, 0)),
        out_shape=jax.ShapeDtypeStruct((n_out, d), F32),
        scratch_shapes=[pltpu.VMEM((t, d), BF16)],
        compiler_params=pltpu.CompilerParams(
            dimension_semantics=("arbitrary",), vmem_limit_bytes=VMEM_LIMIT),
        name="mix_out",
    )(xs, hf, hb, rest, mod, hg, vg, wsp, bsp, wpa, wpb, wo)


def _ffn_kernel(x_ref, mod_ref, g_ref, w1_ref, w3_ref, w2_ref, fg_ref, o_ref, *, n_ctx_tiles, tile_off, final):
    d = x_ref.shape[1]
    i = pl.program_id(0) + tile_off
    mod = jnp.where(i < n_ctx_tiles, mod_ref[1:2, :], mod_ref[0:1, :])
    shift = mod[:, 3 * d:4 * d]
    scale = mod[:, 4 * d:5 * d]
    gate2 = mod[:, 5 * d:6 * d]
    x = x_ref[...]
    hb = (_rms(x, g_ref[...]) * (1.0 + scale) + shift).astype(BF16)
    f = (_silu(_dot(hb, w1_ref[...])) * _dot(hb, w3_ref[...])).astype(BF16)
    y = x + gate2 * _dot(f, w2_ref[...])
    o_ref[...] = _rms(y, fg_ref[...]) if final else y


def _ffn_call(xs, mod, g2, w1, w3, w2, fg, *, n_ctx_tiles, tile_off, final):
    n, d = xs.shape
    t = TILE
    kern = functools.partial(_ffn_kernel, n_ctx_tiles=n_ctx_tiles, tile_off=tile_off, final=final)
    return pl.pallas_call(
        kern,
        grid=(n // t,),
        in_specs=[
            pl.BlockSpec((t, d), lambda i: (i, 0)),
            _const_spec(mod.shape), _const_spec(g2.shape),
            _const_spec(w1.shape), _const_spec(w3.shape), _const_spec(w2.shape), _const_spec(fg.shape),
        ],
        out_specs=pl.BlockSpec((t, d), lambda i: (i, 0)),
        out_shape=jax.ShapeDtypeStruct((n, d), F32),
        compiler_params=pltpu.CompilerParams(
            dimension_semantics=("arbitrary",), vmem_limit_bytes=VMEM_LIMIT),
        name="ffn",
    )(xs, mod, g2, w1, w3, w2, fg)


def _pad_lanes(a, width=LANES):
    return jnp.pad(a, ((0, 0),) * (a.ndim - 1) + ((0, width - a.shape[-1]),))


def kernel(x, c, ctx, c_ctx, w_ada, b_ada, norm1_g, norm2_g, w_in, conv_w, conv_b, gate_b, head_g, vnorm_g,
           w_spatial, b_spatial, w_pa, w_pb, w_out, w_ff1, w_ff3, w_ff2, final_g):
    batch, seq, d = x.shape
    ctx_len = ctx.shape[1]
    depth = w_ada.shape[0]
    assert batch == 1 and ctx_len % TILE == 0 and seq % TILE == 0
    n_ctx_tiles = ctx_len // TILE

    cc = jnp.zeros((8, d), F32).at[0].set(c[0]).at[1].set(c_ctx)
    mod = _ada_call(cc, w_ada, b_ada)

    qk_w, v_w, n_gate = 2 * HEADS * DQK, HEADS * DV, 4 * HEADS
    g0 = qk_w + 2 * v_w
    w_main = jnp.concatenate([w_in[:, :, :g0], w_in[:, :, g0 + n_gate:]], axis=2).astype(BF16)
    w_gate = w_in[:, :, g0:g0 + n_gate].reshape(depth, d, 2, 2, HEADS)
    w_gi = _pad_lanes(w_gate[:, :, :, 0, :].reshape(depth, d, 2 * HEADS)).astype(BF16)
    w_gf = _pad_lanes(w_gate[:, :, :, 1, :].reshape(depth, d, 2 * HEADS)).astype(BF16)
    gb = gate_b.reshape(depth, 2, 2, HEADS)
    gb_i = _pad_lanes(gb[:, :, 0, :].reshape(depth, 1, 2 * HEADS))
    gb_f = _pad_lanes(gb[:, :, 1, :].reshape(depth, 1, 2 * HEADS))
    cw = jnp.pad(conv_w, ((0, 0), (0, 8 - CONV_W), (0, 0)))
    bsp_t = _pad_lanes(jnp.swapaxes(b_spatial, 1, 2))
    wsp = w_spatial.astype(BF16)
    wpa, wpb, wo = w_pa.astype(BF16), w_pb.astype(BF16), w_out.astype(BF16)
    w1, w3, w2 = w_ff1.astype(BF16), w_ff3.astype(BF16), w_ff2.astype(BF16)
    row = lambda a: a.reshape(1, -1)

    xs = jnp.concatenate([ctx[0], x[0]], axis=0)
    for l in range(depth):
        last = l == depth - 1
        off = n_ctx_tiles if last else 0
        q, k, v, rest, col, rowp = _in_proj_call(
            xs, mod[l], row(norm1_g[l]), w_main[l], w_gi[l], w_gf[l], cw[l], row(conv_b[l]), gb_i[l], gb_f[l],
            n_ctx_tiles=n_ctx_tiles)
        hf, hb = _mlstm_call(q, k, v, col, rowp, n_ctx_chunks=ctx_len // CHUNK)
        xs = _mix_out_call(xs, hf, hb, rest, mod[l], row(head_g[l]), row(vnorm_g[l]), wsp[l], bsp_t[l],
                           wpa[l], wpb[l], wo[l], n_ctx_tiles=n_ctx_tiles, tile_off=off)
        xs = _ffn_call(xs, mod[l], row(norm2_g[l]), w1[l], w3[l], w2[l], row(final_g),
                       n_ctx_tiles=n_ctx_tiles, tile_off=off, final=last)
    return xs[None]
```

```python
import functools

import jax
import jax.numpy as jnp
from jax import lax
from jax.experimental import pallas as pl
from jax.experimental.pallas import tpu as pltpu

F32 = jnp.float32
BF16 = jnp.bfloat16
HIGHEST = lax.Precision.HIGHEST

EPS = 1e-6
HEADS = 4
DQK = 128
DV = 256
GROUPS = 4
CHUNK = 128
CONV_W = 5
HALO = 8
TILE = 2 * CHUNK
LANES = 128
VMEM_LIMIT = 56 * 1024 * 1024

NT_DIMS = (((1,), (1,)), ((), ()))
TN_DIMS = (((0,), (0,)), ((), ()))


def _silu(x):
    return x * jax.nn.sigmoid(x)


def _gelu_tanh(x):
    return 0.5 * x * (1.0 + jnp.tanh(0.7978845608028654 * (x + 0.044715 * (x * x * x))))


def _rms(x, g):
    return x * lax.rsqrt(jnp.mean(x * x, axis=-1, keepdims=True) + EPS) * g


def _dot(a, b):
    return jnp.dot(a, b, preferred_element_type=F32)


def _layer_spec(shape, layer, col_block=0):
    return pl.BlockSpec((None,) + tuple(shape), lambda *_: (layer, 0, col_block), pipeline_mode=pl.Buffered(1))


def _params():
    return pltpu.CompilerParams(dimension_semantics=("arbitrary",), vmem_limit_bytes=VMEM_LIMIT)


def _ada_kernel(cc_ref, w_ref, b_ref, o_ref):
    s = _silu(cc_ref[...])
    o_ref[0] = jnp.dot(s, w_ref[0], preferred_element_type=F32, precision=HIGHEST) + b_ref[0]


def _ada_call(cc, w_ada, b_ada):
    depth, d, width = w_ada.shape
    tn = width // 4
    return pl.pallas_call(
        _ada_kernel,
        grid=(depth, width // tn),
        in_specs=[
            pl.BlockSpec((8, d), lambda l, j: (0, 0)),
            pl.BlockSpec((1, d, tn), lambda l, j: (l, 0, j)),
            pl.BlockSpec((1, 1, tn), lambda l, j: (l, 0, j)),
        ],
        out_specs=pl.BlockSpec((1, 8, tn), lambda l, j: (l, 0, j)),
        out_shape=jax.ShapeDtypeStruct((depth, 8, width), F32),
        compiler_params=pltpu.CompilerParams(
            dimension_semantics=("arbitrary", "arbitrary"), vmem_limit_bytes=VMEM_LIMIT),
        name="ada_mod",
    )(cc, w_ada, b_ada.reshape(depth, 1, width))


def _norm_mod(x, g, mod, which, d):
    shift = mod[:, (3 * which) * d:(3 * which + 1) * d]
    scale = mod[:, (3 * which + 1) * d:(3 * which + 2) * d]
    return _rms(x, g) * (1.0 + scale) + shift


def _in_proj_kernel(x_ref, xp_ref, xn_ref, mod_ref, g_ref, w_ref, wgi_ref, wgf_ref, cw_ref, cb_ref,
                    gbi_ref, gbf_ref, q_ref, k_ref, v_ref, col_ref, row_ref, zs_ref,
                    *, n_ctx_tiles, n_tiles):
    d = x_ref.shape[1]
    t = x_ref.shape[0]
    q_w = q_ref.shape[1]
    qk_w = q_w + k_ref.shape[1]
    i = pl.program_id(0)
    mod = jnp.where(i < n_ctx_tiles, mod_ref[1:2, :], mod_ref[0:1, :])
    g = g_ref[...]
    hb = _norm_mod(x_ref[...], g, mod, 0, d).astype(BF16)
    hh = _norm_mod(jnp.concatenate([xp_ref[...], xn_ref[...]], axis=0), g, mod, 0, d).astype(BF16)

    w_qk = w_ref[:, 0:qk_w]
    zh = _dot(hh, w_qk)
    prev_ok = jnp.logical_and(i != 0, i != n_ctx_tiles)
    next_ok = jnp.logical_and(i != n_ctx_tiles - 1, i != n_tiles - 1)
    zs_ref[0:HALO, :] = jnp.where(prev_ok, zh[0:HALO], 0.0)
    zs_ref[HALO:HALO + t, :] = _dot(hb, w_qk)
    zs_ref[HALO + t:2 * HALO + t, :] = jnp.where(next_ok, zh[HALO:2 * HALO], 0.0)
    v_ref[...] = _dot(hb, w_ref[:, qk_w:qk_w + v_ref.shape[1]]).astype(BF16)
    gi = _dot(hb, wgi_ref[...]) + gbi_ref[...]
    gf = _dot(hb, wgf_ref[...]) + gbf_ref[...]

    acc = cb_ref[...]
    for j in range(CONV_W):
        acc = acc + cw_ref[j:j + 1, :] * zs_ref[pl.ds(HALO - CONV_W // 2 + j, t), :]
    qk = _silu(acc)
    q_ref[...] = (qk[:, 0:q_w] * (DQK ** -0.5)).astype(BF16)
    k_ref[...] = qk[:, q_w:].astype(BF16)

    logf = jnp.minimum(gf, 0.0) - jnp.log1p(jnp.exp(-jnp.abs(gf)))
    ti = lax.broadcasted_iota(jnp.int32, (CHUNK, CHUNK), 0)
    si = lax.broadcasted_iota(jnp.int32, (CHUNK, CHUNK), 1)
    lower = si <= ti
    upper = si >= ti
    used = si < 2 * HEADS
    for c in range(t // CHUNK):
        rows = slice(c * CHUNK, (c + 1) * CHUNK)
        lf = logf[rows]
        prefix = jnp.dot(lower.astype(F32), lf, preferred_element_type=F32, precision=HIGHEST)
        suffix = jnp.dot(upper.astype(F32), lf, preferred_element_type=F32, precision=HIGHEST)
        b = jnp.where(used, jnp.where(si < HEADS, prefix, suffix), 0.0)
        a = jnp.where(used, gi[rows] - b, 0.0)
        a_t = a.T
        a_max = jnp.zeros((CHUNK, LANES), F32)
        for hd in range(2 * HEADS):
            msk = lower if hd < HEADS else upper
            run = jnp.max(jnp.where(msk, a_t[hd:hd + 1, :], -jnp.inf), axis=-1, keepdims=True)
            a_max = jnp.where(si == hd, run, a_max)
        col_ref[rows, 0:LANES] = a
        col_ref[rows, LANES:2 * LANES] = a_max
        col_ref[rows, 2 * LANES:3 * LANES] = b
        row_ref[:, rows] = a_t[0:2 * HEADS, :]


def _in_proj_call(xs, mod, g1, w_bf, wgi, wgf, cw, cb, gbi, gbf, *, layer, n_ctx_tiles):
    n, d = xs.shape
    t = TILE
    n_tiles = n // t
    hb = t // HALO
    last_halo = n // HALO - 1
    qk_w = 2 * HEADS * DQK
    v_w = HEADS * DV
    tile = lambda i: (i, 0)
    kern = functools.partial(_in_proj_kernel, n_ctx_tiles=n_ctx_tiles, n_tiles=n_tiles)
    return pl.pallas_call(
        kern,
        grid=(n_tiles,),
        in_specs=[
            pl.BlockSpec((t, d), tile),
            pl.BlockSpec((HALO, d), lambda i: (jnp.maximum(i * hb - 1, 0), 0)),
            pl.BlockSpec((HALO, d), lambda i: (jnp.minimum((i + 1) * hb, last_halo), 0)),
            _layer_spec(mod.shape[1:], layer), _layer_spec(g1.shape[1:], layer),
            _layer_spec((d, qk_w + v_w), layer),
            _layer_spec(wgi.shape[1:], layer), _layer_spec(wgf.shape[1:], layer),
            _layer_spec(cw.shape[1:], layer), _layer_spec(cb.shape[1:], layer),
            _layer_spec(gbi.shape[1:], layer), _layer_spec(gbf.shape[1:], layer),
        ],
        out_specs=[
            pl.BlockSpec((t, qk_w // 2), tile),
            pl.BlockSpec((t, qk_w // 2), tile),
            pl.BlockSpec((t, v_w), tile),
            pl.BlockSpec((t, 3 * LANES), tile),
            pl.BlockSpec((2 * HEADS, t), lambda i: (0, i)),
        ],
        out_shape=[
            jax.ShapeDtypeStruct((n, qk_w // 2), BF16),
            jax.ShapeDtypeStruct((n, qk_w // 2), BF16),
            jax.ShapeDtypeStruct((n, v_w), BF16),
            jax.ShapeDtypeStruct((n, 3 * LANES), F32),
            jax.ShapeDtypeStruct((2 * HEADS, n), F32),
        ],
        scratch_shapes=[pltpu.VMEM((t + 2 * HALO, qk_w), F32)],
        compiler_params=_params(),
        name="in_proj",
    )(xs, xs, xs, mod, g1, w_bf, wgi, wgf, cw, cb, gbi, gbf)


def _mlstm_kernel(qf_ref, kf_ref, vf_ref, colf_ref, rowf_ref, qb_ref, kb_ref, vb_ref, colb_ref, rowb_ref,
                  hf_ref, hb_ref, c_ref, n_ref, m_ref, p_ref, dl_ref, rs_ref, dn_ref):
    chunks = qf_ref.shape[0] // CHUNK

    @pl.when(pl.program_id(0) == 0)
    def _():
        c_ref[...] = jnp.zeros_like(c_ref)
        n_ref[...] = jnp.zeros_like(n_ref)
        m_ref[...] = jnp.zeros_like(m_ref)

    ti = lax.broadcasted_iota(jnp.int32, (CHUNK, CHUNK), 0)
    si = lax.broadcasted_iota(jnp.int32, (CHUNK, CHUNK), 1)
    lane_row = si[0:1, :]
    dirs = ((qf_ref, kf_ref, vf_ref, colf_ref, rowf_ref, hf_ref),
            (qb_ref, kb_ref, vb_ref, colb_ref, rowb_ref, hb_ref))

    for reverse, (q_ref, k_ref, v_ref, col_ref, row_ref, _) in enumerate(dirs):
        mask = (si >= ti) if reverse else (si <= ti)
        end = 0 if reverse else CHUNK - 1
        for c in range(chunks):
            rows = slice(c * CHUNK, (c + 1) * CHUNK)
            slot = reverse * chunks + c
            a_p = col_ref[rows, 0:LANES]
            amax_p = col_ref[rows, LANES:2 * LANES]
            w_loc_p = jnp.exp(a_p - amax_p[end:end + 1, :])
            rs_p = jnp.zeros((CHUNK, LANES), F32)
            for head in range(HEADS):
                hd = reverse * HEADS + head
                idx = slot * HEADS + head
                q = q_ref[rows, head * DQK:(head + 1) * DQK]
                k = k_ref[rows, head * DQK:(head + 1) * DQK]
                v = v_ref[rows, head * DV:(head + 1) * DV]
                decay = jnp.where(mask, jnp.exp(row_ref[hd:hd + 1, rows] - amax_p[:, hd:hd + 1]), 0.0)
                s = lax.dot_general(q, k, NT_DIMS, preferred_element_type=F32) * decay
                p_ref[idx] = _dot(s.astype(BF16), v)
                rs_p = jnp.where(si == hd, jnp.sum(s, axis=-1, keepdims=True), rs_p)
                kw = k.astype(F32) * w_loc_p[:, hd:hd + 1]
                dl_ref[idx] = lax.dot_general(kw.astype(BF16), v, TN_DIMS, preferred_element_type=F32)
                dn_ref[idx:idx + 1, :] = jnp.sum(kw, axis=0, keepdims=True)
            rs_ref[slot] = rs_p

    for reverse, (q_ref, _, _, col_ref, _, h_ref) in enumerate(dirs):
        end = 0 if reverse else CHUNK - 1
        in_dir = jnp.logical_and(lane_row >= reverse * HEADS, lane_row < (reverse + 1) * HEADS)
        for c in (reversed(range(chunks)) if reverse else range(chunks)):
            rows = slice(c * CHUNK, (c + 1) * CHUNK)
            slot = reverse * chunks + c
            amax_p = col_ref[rows, LANES:2 * LANES]
            b_p = col_ref[rows, 2 * LANES:3 * LANES]
            m_row = m_ref[0:1, :]
            m_run = jnp.maximum(amax_p, m_row)
            r_p = jnp.exp(amax_p - m_run)
            w_prev_p = jnp.exp(m_row - m_run)
            qn_p = jnp.zeros((CHUNK, LANES), F32)
            for head in range(HEADS):
                hd = reverse * HEADS + head
                q = q_ref[rows, head * DQK:(head + 1) * DQK]
                qn = jnp.sum(q.astype(F32) * n_ref[hd:hd + 1, :], axis=-1, keepdims=True)
                qn_p = jnp.where(si == hd, qn, qn_p)
            den_p = r_p * rs_ref[slot] + w_prev_p * qn_p
            inv_p = 1.0 / jnp.maximum(jnp.abs(den_p), jnp.exp(-(b_p + m_run)))
            rp = r_p * inv_p
            wp = w_prev_p * inv_p
            m_end = m_run[end:end + 1, :]
            sc_row = jnp.exp(amax_p[end:end + 1, :] - m_end)
            wc_row = jnp.exp(m_row - m_end)
            for head in range(HEADS):
                hd = reverse * HEADS + head
                idx = slot * HEADS + head
                q = q_ref[rows, head * DQK:(head + 1) * DQK]
                c_state = c_ref[hd]
                inter = _dot(q, c_state.astype(BF16))
                h_ref[rows, head * DV:(head + 1) * DV] = (
                    rp[:, hd:hd + 1] * p_ref[idx] + wp[:, hd:hd + 1] * inter)
                wc = wc_row[:, hd:hd + 1]
                sc = sc_row[:, hd:hd + 1]
                c_ref[hd] = wc * c_state + sc * dl_ref[idx]
                n_ref[hd:hd + 1, :] = wc * n_ref[hd:hd + 1, :] + sc * dn_ref[idx:idx + 1, :]
            m_ref[0:1, :] = jnp.where(in_dir, b_p[end:end + 1, :] + m_end, m_row)


def _mlstm_call(q, k, v, col, row, *, n_ctx_tiles):
    n = q.shape[0]
    t = TILE
    n_tiles = n // t
    slots = 2 * (t // CHUNK)
    fwd = lambda j: (j, 0)
    bwd_idx = lambda j: jnp.where(j < n_ctx_tiles, n_ctx_tiles - 1 - j, n_tiles - 1 + n_ctx_tiles - j)
    bwd = lambda j: (bwd_idx(j), 0)
    specs = lambda tok, rowm: [
        pl.BlockSpec((t, q.shape[1]), tok),
        pl.BlockSpec((t, k.shape[1]), tok),
        pl.BlockSpec((t, v.shape[1]), tok),
        pl.BlockSpec((t, col.shape[1]), tok),
        pl.BlockSpec((row.shape[0], t), rowm),
    ]
    return pl.pallas_call(
        _mlstm_kernel,
        grid=(n_tiles,),
        in_specs=specs(fwd, lambda j: (0, j)) + specs(bwd, lambda j: (0, bwd_idx(j))),
        out_specs=[pl.BlockSpec((t, v.shape[1]), fwd), pl.BlockSpec((t, v.shape[1]), bwd)],
        out_shape=[jax.ShapeDtypeStruct((n, v.shape[1]), F32)] * 2,
        scratch_shapes=[
            pltpu.VMEM((2 * HEADS, DQK, DV), F32),
            pltpu.VMEM((2 * HEADS, DQK), F32),
            pltpu.VMEM((8, LANES), F32),
            pltpu.VMEM((slots * HEADS, CHUNK, DV), F32),
            pltpu.VMEM((slots * HEADS, DQK, DV), F32),
            pltpu.VMEM((slots, CHUNK, LANES), F32),
            pltpu.VMEM((slots * HEADS, DQK), F32),
        ],
        compiler_params=_params(),
        name="mlstm",
    )(q, k, v, col, row, q, k, v, col, row)


def _mix_out_kernel(x_ref, hf_ref, hb_ref, mod_ref, g_ref, wog_ref, wr_ref, hg_ref, vg_ref, wsp_ref, bsp_ref,
                    wpa_ref, wpb_ref, wo_ref, o_ref, yb_ref, *, n_ctx_tiles, tile_off):
    d = x_ref.shape[1]
    t = x_ref.shape[0]
    gw = d // GROUPS
    i = pl.program_id(0) + tile_off
    mod = jnp.where(i < n_ctx_tiles, mod_ref[1:2, :], mod_ref[0:1, :])
    gate1 = mod[:, 2 * d:3 * d]
    x = x_ref[...]
    hb = _norm_mod(x, g_ref[...], mod, 0, d).astype(BF16)

    o_gate = jax.nn.sigmoid(_dot(hb, wog_ref[...]))
    ya = []
    for head in range(HEADS):
        cols = slice(head * DV, (head + 1) * DV)
        h = hf_ref[:, cols] + hb_ref[:, cols]
        ya.append((_rms(h, hg_ref[:, cols]) * o_gate[:, cols]).astype(BF16))
    ya = jnp.concatenate(ya, axis=1)
    proj_a = _dot(ya, wpa_ref[...])

    u = _gelu_tanh(_dot(hb, wr_ref[:, 0:d]))
    vn = _rms(_gelu_tanh(_dot(hb, wr_ref[:, d:2 * d])), vg_ref[...]).astype(BF16)
    for c in range(t // CHUNK):
        rows = slice(c * CHUNK, (c + 1) * CHUNK)
        for grp in range(GROUPS):
            cols = slice(grp * gw, (grp + 1) * gw)
            sv = _dot(wsp_ref[grp], vn[rows, cols]) + bsp_ref[:, grp:grp + 1]
            yb_ref[rows, cols] = (u[rows, cols] * sv).astype(BF16)
    proj_b = _dot(yb_ref[...], wpb_ref[...])

    merged = (jax.nn.sigmoid(_dot(hb, wr_ref[:, 2 * d:3 * d])) * proj_a
              + jax.nn.sigmoid(_dot(hb, wr_ref[:, 3 * d:4 * d])) * proj_b)
    o_ref[...] = x + gate1 * _dot(merged.astype(BF16), wo_ref[...])


def _mix_out_call(xs, hf, hb, mod, g1, w_bf, w_rest, hg, vg, wsp, bsp, wpa, wpb, wo, *, layer, n_ctx_tiles,
                  tile_off):
    n, d = xs.shape
    t = TILE
    n_out = n - tile_off * t
    tin = lambda i: (i + tile_off, 0)
    o_block = (2 * HEADS * DQK + HEADS * DV) // d
    kern = functools.partial(_mix_out_kernel, n_ctx_tiles=n_ctx_tiles, tile_off=tile_off)
    return pl.pallas_call(
        kern,
        grid=(n_out // t,),
        in_specs=[
            pl.BlockSpec((t, d), tin), pl.BlockSpec((t, d), tin), pl.BlockSpec((t, d), tin),
            _layer_spec(mod.shape[1:], layer), _layer_spec(g1.shape[1:], layer),
            _layer_spec((d, d), layer, o_block), _layer_spec(w_rest.shape[1:], layer),
            _layer_spec(hg.shape[1:], layer), _layer_spec(vg.shape[1:], layer),
            pl.BlockSpec((None,) + wsp.shape[1:], lambda i: (layer, 0, 0, 0), pipeline_mode=pl.Buffered(1)),
            _layer_spec(bsp.shape[1:], layer),
            _layer_spec(wpa.shape[1:], layer), _layer_spec(wpb.shape[1:], layer), _layer_spec(wo.shape[1:], layer),
        ],
        out_specs=pl.BlockSpec((t, d), lambda i: (i, 0)),
        out_shape=jax.ShapeDtypeStruct((n_out, d), F32),
        scratch_shapes=[pltpu.VMEM((t, d), BF16)],
        compiler_params=_params(),
        name="mix_out",
    )(xs, hf, hb, mod, g1, w_bf, w_rest, hg, vg, wsp, bsp, wpa, wpb, wo)


def _ffn_kernel(x_ref, mod_ref, g_ref, w1_ref, w3_ref, w2_ref, fg_ref, o_ref, *, n_ctx_tiles, tile_off, final):
    d = x_ref.shape[1]
    i = pl.program_id(0) + tile_off
    mod = jnp.where(i < n_ctx_tiles, mod_ref[1:2, :], mod_ref[0:1, :])
    gate2 = mod[:, 5 * d:6 * d]
    x = x_ref[...]
    hb = _norm_mod(x, g_ref[...], mod, 1, d).astype(BF16)
    f = (_silu(_dot(hb, w1_ref[...])) * _dot(hb, w3_ref[...])).astype(BF16)
    y = x + gate2 * _dot(f, w2_ref[...])
    o_ref[...] = _rms(y, fg_ref[...]) if final else y


def _ffn_call(xs, mod, g2, w1, w3, w2, fg, *, layer, n_ctx_tiles, tile_off, final):
    n, d = xs.shape
    t = TILE
    kern = functools.partial(_ffn_kernel, n_ctx_tiles=n_ctx_tiles, tile_off=tile_off, final=final)
    return pl.pallas_call(
        kern,
        grid=(n // t,),
        in_specs=[
            pl.BlockSpec((t, d), lambda i: (i, 0)),
            _layer_spec(mod.shape[1:], layer), _layer_spec(g2.shape[1:], layer),
            _layer_spec(w1.shape[1:], layer), _layer_spec(w3.shape[1:], layer), _layer_spec(w2.shape[1:], layer),
            pl.BlockSpec(fg.shape, lambda i: (0, 0)),
        ],
        out_specs=pl.BlockSpec((t, d), lambda i: (i, 0)),
        out_shape=jax.ShapeDtypeStruct((n, d), F32),
        compiler_params=_params(),
        name="ffn",
    )(xs, mod, g2, w1, w3, w2, fg)


def _pad_lanes(a, width=LANES):
    return jnp.pad(a, ((0, 0),) * (a.ndim - 1) + ((0, width - a.shape[-1]),))


def kernel(x, c, ctx, c_ctx, w_ada, b_ada, norm1_g, norm2_g, w_in, conv_w, conv_b, gate_b, head_g, vnorm_g,
           w_spatial, b_spatial, w_pa, w_pb, w_out, w_ff1, w_ff3, w_ff2, final_g):
    batch, seq, d = x.shape
    ctx_len = ctx.shape[1]
    depth = w_ada.shape[0]
    assert batch == 1 and ctx_len % TILE == 0 and seq % TILE == 0
    n_ctx_tiles = ctx_len // TILE

    cc = jnp.zeros((8, d), F32).at[0].set(c[0]).at[1].set(c_ctx)
    mod = _ada_call(cc, w_ada, b_ada)

    g0 = 2 * HEADS * DQK + 2 * HEADS * DV
    n_gate = 4 * HEADS
    w_bf = w_in.astype(BF16)
    w_rest = w_bf[:, :, g0 + n_gate:]
    w_gate = w_in[:, :, g0:g0 + n_gate].reshape(depth, d, 2, 2, HEADS)
    w_gi = _pad_lanes(w_gate[:, :, :, 0, :].reshape(depth, d, 2 * HEADS)).astype(BF16)
    w_gf = _pad_lanes(w_gate[:, :, :, 1, :].reshape(depth, d, 2 * HEADS)).astype(BF16)
    gb = gate_b.reshape(depth, 2, 2, HEADS)
    gb_i = _pad_lanes(gb[:, :, 0, :].reshape(depth, 1, 2 * HEADS))
    gb_f = _pad_lanes(gb[:, :, 1, :].reshape(depth, 1, 2 * HEADS))
    cw = jnp.pad(conv_w, ((0, 0), (0, 8 - CONV_W), (0, 0)))
    bsp_t = _pad_lanes(jnp.swapaxes(b_spatial, 1, 2))
    wsp = w_spatial.astype(BF16)
    wpa, wpb, wo = w_pa.astype(BF16), w_pb.astype(BF16), w_out.astype(BF16)
    w1, w3, w2 = w_ff1.astype(BF16), w_ff3.astype(BF16), w_ff2.astype(BF16)
    rows = lambda a: a.reshape(depth, 1, -1)
    g1, g2, cb, hg, vg = rows(norm1_g), rows(norm2_g), rows(conv_b), rows(head_g), rows(vnorm_g)

    xs = jnp.concatenate([ctx[0], x[0]], axis=0)
    for l in range(depth):
        last = l == depth - 1
        off = n_ctx_tiles if last else 0
        q, k, v, col, rowp = _in_proj_call(xs, mod, g1, w_bf, w_gi, w_gf, cw, cb, gb_i, gb_f,
                                           layer=l, n_ctx_tiles=n_ctx_tiles)
        hf, hb = _mlstm_call(q, k, v, col, rowp, n_ctx_tiles=n_ctx_tiles)
        xs = _mix_out_call(xs, hf, hb, mod, g1, w_bf, w_rest, hg, vg, wsp, bsp_t, wpa, wpb, wo,
                           layer=l, n_ctx_tiles=n_ctx_tiles, tile_off=off)
        xs = _ffn_call(xs, mod, g2, w1, w3, w2, final_g.reshape(1, -1), layer=l,
                       n_ctx_tiles=n_ctx_tiles, tile_off=off, final=last)
    return xs[None]
```

```python
import functools

import jax
import jax.numpy as jnp
import numpy as np
from jax import lax
from jax.experimental import pallas as pl
from jax.experimental.pallas import tpu as pltpu

F32 = jnp.float32
BF16 = jnp.bfloat16
HIGHEST = lax.Precision.HIGHEST

EPS = 1e-6
HEADS = 4
DQK = 128
DV = 256
GROUPS = 4
CHUNK = 128
CONV_W = 5
HALO = 8
TILE = 2 * CHUNK
LANES = 128
VMEM_LIMIT = 56 * 1024 * 1024
STAGE1_CHUNKS = 2

NT_DIMS = (((1,), (1,)), ((), ()))
TN_DIMS = (((0,), (0,)), ((), ()))


def _silu(x):
    return x * jax.nn.sigmoid(x)


def _gelu_tanh(x):
    return 0.5 * x * (1.0 + jnp.tanh(0.7978845608028654 * (x + 0.044715 * (x * x * x))))


def _rms(x, g):
    return x * lax.rsqrt(jnp.mean(x * x, axis=-1, keepdims=True) + EPS) * g


def _dot(a, b):
    return jnp.dot(a, b, preferred_element_type=F32)


def _layer_spec(shape, layer, col_block=0):
    return pl.BlockSpec((None,) + tuple(shape), lambda *_: (layer, 0, col_block), pipeline_mode=pl.Buffered(1))


def _params():
    return pltpu.CompilerParams(dimension_semantics=("arbitrary",), vmem_limit_bytes=VMEM_LIMIT)


def _ada_kernel(cc_ref, w_ref, b_ref, o_ref):
    s = _silu(cc_ref[...])
    o_ref[0] = jnp.dot(s, w_ref[0], preferred_element_type=F32, precision=HIGHEST) + b_ref[0]


def _ada_call(cc, w_ada, b_ada):
    depth, d, width = w_ada.shape
    tn = width // 2
    return pl.pallas_call(
        _ada_kernel,
        grid=(depth, width // tn),
        in_specs=[
            pl.BlockSpec((8, d), lambda l, j: (0, 0)),
            pl.BlockSpec((1, d, tn), lambda l, j: (l, 0, j)),
            pl.BlockSpec((1, 1, tn), lambda l, j: (l, 0, j)),
        ],
        out_specs=pl.BlockSpec((1, 8, tn), lambda l, j: (l, 0, j)),
        out_shape=jax.ShapeDtypeStruct((depth, 8, width), F32),
        compiler_params=pltpu.CompilerParams(
            dimension_semantics=("arbitrary", "arbitrary"), vmem_limit_bytes=VMEM_LIMIT),
        name="ada_mod",
    )(cc, w_ada, b_ada.reshape(depth, 1, width))


def _norm_mod(x, g, mod, which, d):
    shift = mod[:, (3 * which) * d:(3 * which + 1) * d]
    scale = mod[:, (3 * which + 1) * d:(3 * which + 2) * d]
    return _rms(x, g) * (1.0 + scale) + shift


def _in_proj_kernel(x_ref, xp_ref, xn_ref, mod_ref, g_ref, w_ref, wg_ref, cw_ref, cb_ref, gb_ref, tri_ref,
                    q_ref, k_ref, v_ref, col_ref, row_ref, zs_ref, *, n_ctx_tiles, n_tiles):
    d = x_ref.shape[1]
    t = x_ref.shape[0]
    q_w = q_ref.shape[1]
    qk_w = q_w + k_ref.shape[1]
    i = pl.program_id(0)
    mod = jnp.where(i < n_ctx_tiles, mod_ref[1:2, :], mod_ref[0:1, :])
    x_ext = jnp.concatenate([x_ref[...], xp_ref[...], xn_ref[...]], axis=0)
    h_ext = _norm_mod(x_ext, g_ref[...], mod, 0, d).astype(BF16)
    hb = h_ext[0:t]

    gates = _dot(hb, wg_ref[...]) + gb_ref[...]
    gi = gates[:, 0:LANES]
    gf = gates[:, LANES:2 * LANES]
    logf = jnp.minimum(gf, 0.0) - jnp.log1p(jnp.exp(-jnp.abs(gf)))

    tri = tri_ref[...]
    p0 = logf.astype(BF16)
    r1 = logf - p0.astype(F32)
    p1 = r1.astype(BF16)
    p2 = (r1 - p1.astype(F32)).astype(BF16)
    s01 = _dot(tri, jnp.concatenate([p0, p1], axis=1))
    sums = s01[:, 0:LANES] + s01[:, LANES:2 * LANES] + _dot(tri, p2)

    z_ext = _dot(h_ext, w_ref[:, 0:qk_w])
    prev_ok = jnp.logical_and(i != 0, i != n_ctx_tiles)
    next_ok = jnp.logical_and(i != n_ctx_tiles - 1, i != n_tiles - 1)
    zs_ref[0:HALO, :] = jnp.where(prev_ok, z_ext[t:t + HALO], 0.0)
    zs_ref[HALO:HALO + t, :] = z_ext[0:t]
    zs_ref[HALO + t:2 * HALO + t, :] = jnp.where(next_ok, z_ext[t + HALO:t + 2 * HALO], 0.0)
    v_ref[...] = _dot(hb, w_ref[:, qk_w:qk_w + v_ref.shape[1]]).astype(BF16)

    ti = lax.broadcasted_iota(jnp.int32, (CHUNK, CHUNK), 0)
    si = lax.broadcasted_iota(jnp.int32, (CHUNK, CHUNK), 1)
    lower = si <= ti
    upper = si >= ti
    used = si < 2 * HEADS
    for c in range(t // CHUNK):
        rows = slice(c * CHUNK, (c + 1) * CHUNK)
        prefix = sums[c * CHUNK:(c + 1) * CHUNK]
        suffix = sums[t + c * CHUNK:t + (c + 1) * CHUNK]
        b = jnp.where(used, jnp.where(si < HEADS, prefix, suffix), 0.0)
        a = jnp.where(used, gi[rows] - b, 0.0)
        a_t = a.T
        a_max = jnp.zeros((CHUNK, LANES), F32)
        for hd in range(2 * HEADS):
            msk = lower if hd < HEADS else upper
            run = jnp.max(jnp.where(msk, a_t[hd:hd + 1, :], -jnp.inf), axis=-1, keepdims=True)
            a_max = jnp.where(si == hd, run, a_max)
        col_ref[rows, 0:LANES] = a
        col_ref[rows, LANES:2 * LANES] = a_max
        col_ref[rows, 2 * LANES:3 * LANES] = b
        row_ref[:, rows] = a_t[0:2 * HEADS, :]

    acc = cb_ref[...]
    for j in range(CONV_W):
        acc = acc + cw_ref[j:j + 1, :] * zs_ref[pl.ds(HALO - CONV_W // 2 + j, t), :]
    qk = _silu(acc)
    q_ref[...] = (qk[:, 0:q_w] * (DQK ** -0.5)).astype(BF16)
    k_ref[...] = qk[:, q_w:].astype(BF16)


def _in_proj_call(xs, mod, g1, w_qkvo, wg, cw, cb, gb, *, layer, n_ctx_tiles):
    n, d = xs.shape
    t = TILE
    n_tiles = n // t
    hb = t // HALO
    last_halo = n // HALO - 1
    qk_w = 2 * HEADS * DQK
    v_w = HEADS * DV
    tile = lambda i: (i, 0)
    tok = np.arange(t)
    same_chunk = (tok[:, None] // CHUNK) == (tok[None, :] // CHUNK)
    tri = jnp.asarray(np.concatenate([same_chunk & (tok[None, :] <= tok[:, None]),
                                      same_chunk & (tok[None, :] >= tok[:, None])], axis=0), BF16)
    kern = functools.partial(_in_proj_kernel, n_ctx_tiles=n_ctx_tiles, n_tiles=n_tiles)
    return pl.pallas_call(
        kern,
        grid=(n_tiles,),
        in_specs=[
            pl.BlockSpec((t, d), tile),
            pl.BlockSpec((HALO, d), lambda i: (jnp.maximum(i * hb - 1, 0), 0)),
            pl.BlockSpec((HALO, d), lambda i: (jnp.minimum((i + 1) * hb, last_halo), 0)),
            _layer_spec(mod.shape[1:], layer), _layer_spec(g1.shape[1:], layer),
            _layer_spec((d, qk_w + v_w), layer), _layer_spec(wg.shape[1:], layer),
            _layer_spec(cw.shape[1:], layer), _layer_spec(cb.shape[1:], layer), _layer_spec(gb.shape[1:], layer),
            pl.BlockSpec(tri.shape, lambda i: (0, 0), pipeline_mode=pl.Buffered(1)),
        ],
        out_specs=[
            pl.BlockSpec((t, qk_w // 2), tile),
            pl.BlockSpec((t, qk_w // 2), tile),
            pl.BlockSpec((t, v_w), tile),
            pl.BlockSpec((t, 3 * LANES), tile),
            pl.BlockSpec((2 * HEADS, t), lambda i: (0, i)),
        ],
        out_shape=[
            jax.ShapeDtypeStruct((n, qk_w // 2), BF16),
            jax.ShapeDtypeStruct((n, qk_w // 2), BF16),
            jax.ShapeDtypeStruct((n, v_w), BF16),
            jax.ShapeDtypeStruct((n, 3 * LANES), F32),
            jax.ShapeDtypeStruct((2 * HEADS, n), F32),
        ],
        scratch_shapes=[pltpu.VMEM((t + 2 * HALO, qk_w), F32)],
        compiler_params=_params(),
        name="in_proj",
    )(xs, xs, xs, mod, g1, w_qkvo, wg, cw, cb, gb, tri)


def _mlstm_kernel(qf_ref, kf_ref, vf_ref, colf_ref, rowf_ref, qb_ref, kb_ref, vb_ref, colb_ref, rowb_ref,
                  hf_ref, hb_ref, c_ref, n_ref, m_ref, p_ref, dl_ref, rs_ref, dn_ref):
    chunks = qf_ref.shape[0] // CHUNK

    @pl.when(pl.program_id(0) == 0)
    def _():
        c_ref[...] = jnp.zeros_like(c_ref)
        n_ref[...] = jnp.zeros_like(n_ref)
        m_ref[...] = jnp.zeros_like(m_ref)

    ti = lax.broadcasted_iota(jnp.int32, (CHUNK, CHUNK), 0)
    si = lax.broadcasted_iota(jnp.int32, (CHUNK, CHUNK), 1)
    lane_row = si[0:1, :]
    dirs = ((qf_ref, kf_ref, vf_ref, colf_ref, rowf_ref, hf_ref),
            (qb_ref, kb_ref, vb_ref, colb_ref, rowb_ref, hb_ref))

    heads = range(HEADS)
    qk_cols = lambda head: slice(head * DQK, (head + 1) * DQK)
    v_cols = lambda head: slice(head * DV, (head + 1) * DV)

    masks = [si <= ti, si >= ti]
    ends = [CHUNK - 1, 0]
    for c0 in range(0, chunks, STAGE1_CHUNKS):
        slots = [(c, r) for c in range(c0, c0 + STAGE1_CHUNKS) for r in range(2)]
        group = [(c, r, h) for c, r in slots for h in heads]
        rows = lambda c: slice(c * CHUNK, (c + 1) * CHUNK)
        amax_p = {(c, r): dirs[r][3][rows(c), LANES:2 * LANES] for c, r in slots}
        w_loc_p = {(c, r): jnp.exp(dirs[r][3][rows(c), 0:LANES] - amax_p[c, r][ends[r]:ends[r] + 1, :])
                   for c, r in slots}
        s_raw = [lax.dot_general(dirs[r][0][rows(c), qk_cols(h)], dirs[r][1][rows(c), qk_cols(h)], NT_DIMS,
                                 preferred_element_type=F32) for c, r, h in group]
        kw = [dirs[r][1][rows(c), qk_cols(h)].astype(F32) * w_loc_p[c, r][:, r * HEADS + h:r * HEADS + h + 1]
              for c, r, h in group]
        for j, (c, r, h) in enumerate(group):
            idx = (r * chunks + c) * HEADS + h
            dl_ref[idx] = lax.dot_general(kw[j].astype(BF16), dirs[r][2][rows(c), v_cols(h)], TN_DIMS,
                                          preferred_element_type=F32)
            dn_ref[idx:idx + 1, :] = jnp.sum(kw[j], axis=0, keepdims=True)
        s = []
        for j, (c, r, h) in enumerate(group):
            hd = r * HEADS + h
            decay = jnp.where(masks[r], jnp.exp(dirs[r][4][hd:hd + 1, rows(c)] - amax_p[c, r][:, hd:hd + 1]), 0.0)
            s.append(s_raw[j] * decay)
        for j, (c, r, h) in enumerate(group):
            idx = (r * chunks + c) * HEADS + h
            p_ref[idx] = _dot(s[j].astype(BF16), dirs[r][2][rows(c), v_cols(h)])
        for c, r in slots:
            rs_p = jnp.zeros((CHUNK, LANES), F32)
            for j, (cc, rr, h) in enumerate(group):
                if (cc, rr) == (c, r):
                    rs_p = jnp.where(si == r * HEADS + h, jnp.sum(s[j], axis=-1, keepdims=True), rs_p)
            rs_ref[r * chunks + c] = rs_p

    m_rows = [m_ref[0:1, :], m_ref[1:2, :]]
    for step in range(chunks):
        cs = [step, chunks - 1 - step]
        group = [(reverse, head) for reverse in range(2) for head in heads]
        rp, wp, sc_row, wc_row = [], [], [], []
        for reverse in range(2):
            q_ref, col_ref = dirs[reverse][0], dirs[reverse][3]
            end = 0 if reverse else CHUNK - 1
            in_dir = jnp.logical_and(lane_row >= reverse * HEADS, lane_row < (reverse + 1) * HEADS)
            rows = slice(cs[reverse] * CHUNK, (cs[reverse] + 1) * CHUNK)
            amax_p = col_ref[rows, LANES:2 * LANES]
            b_p = col_ref[rows, 2 * LANES:3 * LANES]
            m_row = m_rows[reverse]
            m_run = jnp.maximum(amax_p, m_row)
            r_p = jnp.exp(amax_p - m_run)
            w_prev_p = jnp.exp(m_row - m_run)
            qn_p = jnp.zeros((CHUNK, LANES), F32)
            for head in heads:
                hd = reverse * HEADS + head
                qn = jnp.sum(q_ref[rows, qk_cols(head)].astype(F32) * n_ref[hd:hd + 1, :], axis=-1, keepdims=True)
                qn_p = jnp.where(si == hd, qn, qn_p)
            den_p = r_p * rs_ref[reverse * chunks + cs[reverse]] + w_prev_p * qn_p
            inv_p = 1.0 / jnp.maximum(jnp.abs(den_p), jnp.exp(-(b_p + m_run)))
            rp.append(r_p * inv_p)
            wp.append(w_prev_p * inv_p)
            m_end = m_run[end:end + 1, :]
            sc_row.append(jnp.exp(amax_p[end:end + 1, :] - m_end))
            wc_row.append(jnp.exp(m_row - m_end))
            m_rows[reverse] = jnp.where(in_dir, b_p[end:end + 1, :] + m_end, m_row)
        c_state = [c_ref[r * HEADS + h] for r, h in group]
        inter = [_dot(dirs[r][0][slice(cs[r] * CHUNK, (cs[r] + 1) * CHUNK), qk_cols(h)], c_state[j].astype(BF16))
                 for j, (r, h) in enumerate(group)]
        for j, (r, h) in enumerate(group):
            hd = r * HEADS + h
            idx = (r * chunks + cs[r]) * HEADS + h
            wc = wc_row[r][:, hd:hd + 1]
            sc = sc_row[r][:, hd:hd + 1]
            c_ref[hd] = wc * c_state[j] + sc * dl_ref[idx]
            n_ref[hd:hd + 1, :] = wc * n_ref[hd:hd + 1, :] + sc * dn_ref[idx:idx + 1, :]
        for j, (r, h) in enumerate(group):
            hd = r * HEADS + h
            idx = (r * chunks + cs[r]) * HEADS + h
            rows = slice(cs[r] * CHUNK, (cs[r] + 1) * CHUNK)
            dirs[r][5][rows, v_cols(h)] = rp[r][:, hd:hd + 1] * p_ref[idx] + wp[r][:, hd:hd + 1] * inter[j]
    m_ref[0:1, :] = m_rows[0]
    m_ref[1:2, :] = m_rows[1]


def _mlstm_call(q, k, v, col, row, *, n_ctx_tiles):
    n = q.shape[0]
    t = TILE
    n_tiles = n // t
    slots = 2 * (t // CHUNK)
    fwd = lambda j: (j, 0)
    bwd_idx = lambda j: jnp.where(j < n_ctx_tiles, n_ctx_tiles - 1 - j, n_tiles - 1 + n_ctx_tiles - j)
    bwd = lambda j: (bwd_idx(j), 0)
    specs = lambda tok, rowm: [
        pl.BlockSpec((t, q.shape[1]), tok),
        pl.BlockSpec((t, k.shape[1]), tok),
        pl.BlockSpec((t, v.shape[1]), tok),
        pl.BlockSpec((t, col.shape[1]), tok),
        pl.BlockSpec((row.shape[0], t), rowm),
    ]
    return pl.pallas_call(
        _mlstm_kernel,
        grid=(n_tiles,),
        in_specs=specs(fwd, lambda j: (0, j)) + specs(bwd, lambda j: (0, bwd_idx(j))),
        out_specs=[pl.BlockSpec((t, v.shape[1]), fwd), pl.BlockSpec((t, v.shape[1]), bwd)],
        out_shape=[jax.ShapeDtypeStruct((n, v.shape[1]), F32)] * 2,
        scratch_shapes=[
            pltpu.VMEM((2 * HEADS, DQK, DV), F32),
            pltpu.VMEM((2 * HEADS, DQK), F32),
            pltpu.VMEM((8, LANES), F32),
            pltpu.VMEM((slots * HEADS, CHUNK, DV), F32),
            pltpu.VMEM((slots * HEADS, DQK, DV), F32),
            pltpu.VMEM((slots, CHUNK, LANES), F32),
            pltpu.VMEM((slots * HEADS, DQK), F32),
        ],
        compiler_params=_params(),
        name="mlstm",
    )(q, k, v, col, row, q, k, v, col, row)


def _mix_out_kernel(x_ref, hf_ref, hb_ref, mod_ref, g_ref, wog_ref, wr_ref, hg_ref, vg_ref, wsp_ref, bsp_ref,
                    wpa_ref, wpb_ref, wo_ref, o_ref, yb_ref, *, n_ctx_tiles, tile_off):
    d = x_ref.shape[1]
    t = x_ref.shape[0]
    gw = d // GROUPS
    i = pl.program_id(0) + tile_off
    mod = jnp.where(i < n_ctx_tiles, mod_ref[1:2, :], mod_ref[0:1, :])
    gate1 = mod[:, 2 * d:3 * d]
    x = x_ref[...]
    hb = _norm_mod(x, g_ref[...], mod, 0, d).astype(BF16)

    z_vb = _dot(hb, wr_ref[:, d:2 * d])
    z_u = _dot(hb, wr_ref[:, 0:d])
    vn = _rms(_gelu_tanh(z_vb), vg_ref[...]).astype(BF16)
    z_o = _dot(hb, wog_ref[...])
    u = _gelu_tanh(z_u)

    for c in range(t // CHUNK):
        rows = slice(c * CHUNK, (c + 1) * CHUNK)
        for grp in range(GROUPS):
            cols = slice(grp * gw, (grp + 1) * gw)
            sv = _dot(wsp_ref[grp], vn[rows, cols]) + bsp_ref[:, grp:grp + 1]
            yb_ref[rows, cols] = (u[rows, cols] * sv).astype(BF16)
    z_ga = _dot(hb, wr_ref[:, 2 * d:3 * d])

    o_gate = jax.nn.sigmoid(z_o)
    ya = []
    for head in range(HEADS):
        cols = slice(head * DV, (head + 1) * DV)
        h = hf_ref[:, cols] + hb_ref[:, cols]
        ya.append((_rms(h, hg_ref[:, cols]) * o_gate[:, cols]).astype(BF16))
    ya = jnp.concatenate(ya, axis=1)
    proj_b = _dot(yb_ref[...], wpb_ref[...])
    z_gb = _dot(hb, wr_ref[:, 3 * d:4 * d])
    proj_a = _dot(ya, wpa_ref[...])

    merged = jax.nn.sigmoid(z_ga) * proj_a + jax.nn.sigmoid(z_gb) * proj_b
    o_ref[...] = x + gate1 * _dot(merged.astype(BF16), wo_ref[...])


def _mix_out_call(xs, hf, hb, mod, g1, w_qkvo, w_rest, hg, vg, wsp, bsp, wpa, wpb, wo, *, layer, n_ctx_tiles,
                  tile_off):
    n, d = xs.shape
    t = TILE
    n_out = n - tile_off * t
    tin = lambda i: (i + tile_off, 0)
    o_block = (2 * HEADS * DQK + HEADS * DV) // d
    kern = functools.partial(_mix_out_kernel, n_ctx_tiles=n_ctx_tiles, tile_off=tile_off)
    return pl.pallas_call(
        kern,
        grid=(n_out // t,),
        in_specs=[
            pl.BlockSpec((t, d), tin), pl.BlockSpec((t, d), tin), pl.BlockSpec((t, d), tin),
            _layer_spec(mod.shape[1:], layer), _layer_spec(g1.shape[1:], layer),
            _layer_spec((d, d), layer, o_block), _layer_spec(w_rest.shape[1:], layer),
            _layer_spec(hg.shape[1:], layer), _layer_spec(vg.shape[1:], layer),
            pl.BlockSpec((None,) + wsp.shape[1:], lambda i: (layer, 0, 0, 0), pipeline_mode=pl.Buffered(1)),
            _layer_spec(bsp.shape[1:], layer),
            _layer_spec(wpa.shape[1:], layer), _layer_spec(wpb.shape[1:], layer), _layer_spec(wo.shape[1:], layer),
        ],
        out_specs=pl.BlockSpec((t, d), lambda i: (i, 0)),
        out_shape=jax.ShapeDtypeStruct((n_out, d), F32),
        scratch_shapes=[pltpu.VMEM((t, d), BF16)],
        compiler_params=_params(),
        name="mix_out",
    )(xs, hf, hb, mod, g1, w_qkvo, w_rest, hg, vg, wsp, bsp, wpa, wpb, wo)


def _ffn_kernel(x_ref, mod_ref, g_ref, w1_ref, w3_ref, w2_ref, fg_ref, o_ref, *, n_ctx_tiles, tile_off, final):
    d = x_ref.shape[1]
    i = pl.program_id(0) + tile_off
    mod = jnp.where(i < n_ctx_tiles, mod_ref[1:2, :], mod_ref[0:1, :])
    gate2 = mod[:, 5 * d:6 * d]
    x = x_ref[...]
    hb = _norm_mod(x, g_ref[...], mod, 1, d).astype(BF16)
    f = (_silu(_dot(hb, w1_ref[...])) * _dot(hb, w3_ref[...])).astype(BF16)
    y = x + gate2 * _dot(f, w2_ref[...])
    o_ref[...] = _rms(y, fg_ref[...]) if final else y


def _ffn_call(xs, mod, g2, w1, w3, w2, fg, *, layer, n_ctx_tiles, tile_off, final):
    n, d = xs.shape
    t = TILE
    kern = functools.partial(_ffn_kernel, n_ctx_tiles=n_ctx_tiles, tile_off=tile_off, final=final)
    return pl.pallas_call(
        kern,
        grid=(n // t,),
        in_specs=[
            pl.BlockSpec((t, d), lambda i: (i, 0)),
            _layer_spec(mod.shape[1:], layer), _layer_spec(g2.shape[1:], layer),
            _layer_spec(w1.shape[1:], layer), _layer_spec(w3.shape[1:], layer), _layer_spec(w2.shape[1:], layer),
            pl.BlockSpec(fg.shape, lambda i: (0, 0)),
        ],
        out_specs=pl.BlockSpec((t, d), lambda i: (i, 0)),
        out_shape=jax.ShapeDtypeStruct((n, d), F32),
        compiler_params=_params(),
        name="ffn",
    )(xs, mod, g2, w1, w3, w2, fg)


def _pad_lanes(a, width=LANES):
    return jnp.pad(a, ((0, 0),) * (a.ndim - 1) + ((0, width - a.shape[-1]),))


def kernel(x, c, ctx, c_ctx, w_ada, b_ada, norm1_g, norm2_g, w_in, conv_w, conv_b, gate_b, head_g, vnorm_g,
           w_spatial, b_spatial, w_pa, w_pb, w_out, w_ff1, w_ff3, w_ff2, final_g):
    batch, seq, d = x.shape
    ctx_len = ctx.shape[1]
    depth = w_ada.shape[0]
    assert batch == 1 and ctx_len % TILE == 0 and seq % TILE == 0
    n_ctx_tiles = ctx_len // TILE

    cc = jnp.zeros((8, d), F32).at[0].set(c[0]).at[1].set(c_ctx)
    mod = _ada_call(cc, w_ada, b_ada)

    g0 = 2 * HEADS * DQK + 2 * HEADS * DV
    n_gate = 4 * HEADS
    w_qkvo = w_in[:, :, :g0].astype(BF16)
    w_rest = w_in[:, :, g0 + n_gate:].astype(BF16)
    w_gate = w_in[:, :, g0:g0 + n_gate].reshape(depth, d, 2, 2, HEADS)
    w_g = jnp.concatenate([_pad_lanes(w_gate[:, :, :, j, :].reshape(depth, d, 2 * HEADS)) for j in range(2)],
                          axis=2).astype(BF16)
    gb4 = gate_b.reshape(depth, 2, 2, HEADS)
    gb = jnp.concatenate([_pad_lanes(gb4[:, :, j, :].reshape(depth, 1, 2 * HEADS)) for j in range(2)], axis=2)
    cw = jnp.pad(conv_w, ((0, 0), (0, 8 - CONV_W), (0, 0)))
    bsp_t = _pad_lanes(jnp.swapaxes(b_spatial, 1, 2))
    wsp = w_spatial.astype(BF16)
    wpa, wpb, wo = w_pa.astype(BF16), w_pb.astype(BF16), w_out.astype(BF16)
    w1, w3, w2 = w_ff1.astype(BF16), w_ff3.astype(BF16), w_ff2.astype(BF16)
    rows = lambda a: a.reshape(depth, 1, -1)
    g1, g2, cb, hg, vg = rows(norm1_g), rows(norm2_g), rows(conv_b), rows(head_g), rows(vnorm_g)

    xs = jnp.concatenate([ctx[0], x[0]], axis=0)
    for l in range(depth):
        last = l == depth - 1
        off = n_ctx_tiles if last else 0
        q, k, v, col, rowp = _in_proj_call(xs, mod, g1, w_qkvo, w_g, cw, cb, gb,
                                           layer=l, n_ctx_tiles=n_ctx_tiles)
        hf, hb = _mlstm_call(q, k, v, col, rowp, n_ctx_tiles=n_ctx_tiles)
        xs = _mix_out_call(xs, hf, hb, mod, g1, w_qkvo, w_rest, hg, vg, wsp, bsp_t, wpa, wpb, wo,
                           layer=l, n_ctx_tiles=n_ctx_tiles, tile_off=off)
        xs = _ffn_call(xs, mod, g2, w1, w3, w2, final_g.reshape(1, -1), layer=l,
                       n_ctx_tiles=n_ctx_tiles, tile_off=off, final=last)
    return xs[None]
```

```python
import functools

import jax
import jax.numpy as jnp
import numpy as np
from jax import lax
from jax.experimental import pallas as pl
from jax.experimental.pallas import tpu as pltpu

F32 = jnp.float32
BF16 = jnp.bfloat16
HIGHEST = lax.Precision.HIGHEST

EPS = 1e-6
HEADS = 4
DQK = 128
DV = 256
GROUPS = 4
CHUNK = 128
CONV_W = 5
HALO = 8
TILE = 2 * CHUNK
LANES = 128
VMEM_LIMIT = 56 * 1024 * 1024
COL_PIECE = 256
STAGE1_CHUNKS = 2

NT_DIMS = (((1,), (1,)), ((), ()))
TN_DIMS = (((0,), (0,)), ((), ()))


def _silu(x):
    return x * jax.nn.sigmoid(x)


def _gelu_tanh(x):
    return 0.5 * x * (1.0 + jnp.tanh(0.7978845608028654 * (x + 0.044715 * (x * x * x))))


def _rms(x, g):
    return x * lax.rsqrt(jnp.mean(x * x, axis=-1, keepdims=True) + EPS) * g


def _dot(a, b):
    return jnp.dot(a, b, preferred_element_type=F32)


def _layer_spec(shape, layer, col_block=0):
    return pl.BlockSpec((None,) + tuple(shape), lambda *_: (layer, 0, col_block), pipeline_mode=pl.Buffered(1))


def _params():
    return pltpu.CompilerParams(dimension_semantics=("arbitrary",), vmem_limit_bytes=VMEM_LIMIT)


def _ada_kernel(ct_ref, w_ref, b_ref, o_ref):
    s = _silu(ct_ref[...])
    w = w_ref[0]
    rows = [jnp.sum(w * s[:, r:r + 1], axis=0, keepdims=True) for r in range(2)]
    pad = jnp.zeros((o_ref.shape[1] - 2, w.shape[1]), F32)
    o_ref[0] = jnp.concatenate(rows + [pad], axis=0) + b_ref[0]


def _ada_call(cc_t, w_ada, b_ada):
    depth, d, width = w_ada.shape
    tn = width // 2
    return pl.pallas_call(
        _ada_kernel,
        grid=(depth, width // tn),
        in_specs=[
            pl.BlockSpec((d, 8), lambda l, j: (0, 0)),
            pl.BlockSpec((1, d, tn), lambda l, j: (l, 0, j)),
            pl.BlockSpec((1, 1, tn), lambda l, j: (l, 0, j)),
        ],
        out_specs=pl.BlockSpec((1, 8, tn), lambda l, j: (l, 0, j)),
        out_shape=jax.ShapeDtypeStruct((depth, 8, width), F32),
        compiler_params=pltpu.CompilerParams(
            dimension_semantics=("arbitrary", "arbitrary"), vmem_limit_bytes=VMEM_LIMIT),
        name="ada_mod",
    )(cc_t, w_ada, b_ada.reshape(depth, 1, width))


def _split_cast_kernel(w_ref, a_ref, b_ref):
    a_ref[0] = w_ref[0, :, 0:a_ref.shape[2]].astype(BF16)
    b_ref[0] = w_ref[0, :, w_ref.shape[2] - b_ref.shape[2]:].astype(BF16)


def _split_cast_call(w, head_w, tail_w, row_block=256):
    depth, d, width = w.shape
    return pl.pallas_call(
        _split_cast_kernel,
        grid=(depth, d // row_block),
        in_specs=[pl.BlockSpec((1, row_block, width), lambda l, r: (l, r, 0))],
        out_specs=[pl.BlockSpec((1, row_block, head_w), lambda l, r: (l, r, 0)),
                   pl.BlockSpec((1, row_block, tail_w), lambda l, r: (l, r, 0))],
        out_shape=[jax.ShapeDtypeStruct((depth, d, head_w), BF16),
                   jax.ShapeDtypeStruct((depth, d, tail_w), BF16)],
        compiler_params=pltpu.CompilerParams(
            dimension_semantics=("arbitrary", "arbitrary"), vmem_limit_bytes=VMEM_LIMIT),
        name="split_cast",
    )(w)


def _norm_mod(x, g, mod, which, d):
    shift = mod[:, (3 * which) * d:(3 * which + 1) * d]
    scale = mod[:, (3 * which + 1) * d:(3 * which + 2) * d]
    return _rms(x, g) * (1.0 + scale) + shift


def _in_proj_kernel(*refs, n_ctx_tiles, n_tiles, two_sources):
    n_tok = 6 if two_sources else 3
    tok_refs, refs = refs[:n_tok], refs[n_tok:]
    (mod_ref, g_ref, w_ref, wg_ref, cw_ref, cb_ref, gb_ref, tri_ref,
     q_ref, k_ref, v_ref, col_ref, row_ref), zs_refs = refs[:13], refs[13:]
    t, d = tok_refs[0].shape
    q_w = q_ref.shape[1]
    qk_w = q_w + k_ref.shape[1]
    i = pl.program_id(0)
    is_ctx = i < n_ctx_tiles
    mod = jnp.where(is_ctx, mod_ref[1:2, :], mod_ref[0:1, :])
    x_ext = jnp.concatenate([r[...] for r in tok_refs[-3:]], axis=0)
    if two_sources:
        x_ext = jnp.where(is_ctx, jnp.concatenate([r[...] for r in tok_refs[:3]], axis=0), x_ext)
    h_ext = _norm_mod(x_ext, g_ref[...], mod, 0, d).astype(BF16)
    hb = h_ext[0:t]

    gates = _dot(hb, wg_ref[...]) + gb_ref[...]
    gi = gates[:, 0:LANES]
    gf = gates[:, LANES:2 * LANES]
    logf = jnp.minimum(gf, 0.0) - jnp.log1p(jnp.exp(-jnp.abs(gf)))

    tri = tri_ref[...]
    p0 = logf.astype(BF16)
    r1 = logf - p0.astype(F32)
    p1 = r1.astype(BF16)
    p2 = (r1 - p1.astype(F32)).astype(BF16)
    s01 = _dot(tri, jnp.concatenate([p0, p1], axis=1))
    sums = s01[:, 0:LANES] + s01[:, LANES:2 * LANES] + _dot(tri, p2)

    prev_ok = jnp.logical_and(i != 0, i != n_ctx_tiles)
    next_ok = jnp.logical_and(i != n_ctx_tiles - 1, i != n_tiles - 1)
    half = CONV_W // 2
    n_groups = t // HALO
    sub = lax.broadcasted_iota(jnp.int32, (HALO, COL_PIECE), 0)

    def project_piece(p):
        cols = slice(p * COL_PIECE, (p + 1) * COL_PIECE)
        z_ext = _dot(h_ext, w_ref[:, cols])
        zs_ref = zs_refs[p]
        zs_ref[0:HALO, :] = jnp.where(prev_ok, z_ext[t:t + HALO], 0.0)
        zs_ref[HALO:HALO + t, :] = z_ext[0:t]
        zs_ref[HALO + t:2 * HALO + t, :] = jnp.where(next_ok, z_ext[t + HALO:t + 2 * HALO], 0.0)

    def conv_piece(p):
        cols = slice(p * COL_PIECE, (p + 1) * COL_PIECE)
        groups = [zs_refs[p][m * HALO:(m + 1) * HALO, :] for m in range(n_groups + 2)]
        rot = {o: [pltpu.roll(gm, (HALO - o) % HALO, axis=0) for gm in groups]
               for o in range(-half, half + 1) if o != 0}
        accs = []
        for gidx in range(n_groups):
            acc = cb_ref[:, cols] + cw_ref[half:half + 1, cols] * groups[gidx + 1]
            for o in range(1, half + 1):
                ahead = jnp.where(sub < HALO - o, rot[o][gidx + 1], rot[o][gidx + 2])
                behind = jnp.where(sub >= o, rot[-o][gidx + 1], rot[-o][gidx])
                acc = (acc + cw_ref[half + o:half + o + 1, cols] * ahead
                       + cw_ref[half - o:half - o + 1, cols] * behind)
            accs.append(acc)
        qk = _silu(jnp.concatenate(accs, axis=0))
        if p * COL_PIECE < q_w:
            q_ref[:, cols] = (qk * (DQK ** -0.5)).astype(BF16)
        else:
            k_ref[:, p * COL_PIECE - q_w:(p + 1) * COL_PIECE - q_w] = qk.astype(BF16)

    def value_piece(p):
        cols = slice(p * COL_PIECE, (p + 1) * COL_PIECE)
        v_ref[:, cols] = _dot(hb, w_ref[:, qk_w + p * COL_PIECE:qk_w + (p + 1) * COL_PIECE]).astype(BF16)

    n_qk = qk_w // COL_PIECE
    n_v = v_ref.shape[1] // COL_PIECE
    project_piece(0)
    for p in range(1, n_qk):
        project_piece(p)
        conv_piece(p - 1)
    value_piece(0)
    conv_piece(n_qk - 1)

    ti = lax.broadcasted_iota(jnp.int32, (CHUNK, CHUNK), 0)
    si = lax.broadcasted_iota(jnp.int32, (CHUNK, CHUNK), 1)
    lower = si <= ti
    upper = si >= ti
    used = si < 2 * HEADS
    for c in range(t // CHUNK):
        value_piece(1 + c)
        rows = slice(c * CHUNK, (c + 1) * CHUNK)
        prefix = sums[c * CHUNK:(c + 1) * CHUNK]
        suffix = sums[t + c * CHUNK:t + (c + 1) * CHUNK]
        b = jnp.where(used, jnp.where(si < HEADS, prefix, suffix), 0.0)
        a = jnp.where(used, gi[rows] - b, 0.0)
        a_t = a.T
        a_max = jnp.zeros((CHUNK, LANES), F32)
        for hd in range(2 * HEADS):
            msk = lower if hd < HEADS else upper
            run = jnp.max(jnp.where(msk, a_t[hd:hd + 1, :], -jnp.inf), axis=-1, keepdims=True)
            a_max = jnp.where(si == hd, run, a_max)
        col_ref[rows, 0:LANES] = a
        col_ref[rows, LANES:2 * LANES] = a_max
        col_ref[rows, 2 * LANES:3 * LANES] = b
        row_ref[:, rows] = a_t[0:2 * HEADS, :]
    for p in range(1 + t // CHUNK, n_v):
        value_piece(p)


def _tile_and_halo_specs(rows, d, tile_of):
    per_tile = TILE // HALO
    last_halo = rows // HALO - 1
    return [
        pl.BlockSpec((TILE, d), lambda i: (tile_of(i), 0)),
        pl.BlockSpec((HALO, d), lambda i: (jnp.maximum(tile_of(i) * per_tile - 1, 0), 0)),
        pl.BlockSpec((HALO, d), lambda i: (jnp.minimum((tile_of(i) + 1) * per_tile, last_halo), 0)),
    ]


def _in_proj_call(tokens, mod, g1, w_qkvo, wg, cw, cb, gb, *, layer, n_ctx_tiles):
    two_sources = isinstance(tokens, tuple)
    t = TILE
    if two_sources:
        ctx_rows, lat_rows = tokens
        d = lat_rows.shape[1]
        n = ctx_rows.shape[0] + lat_rows.shape[0]
        tok_specs = (_tile_and_halo_specs(ctx_rows.shape[0], d, lambda i: jnp.minimum(i, n_ctx_tiles - 1))
                     + _tile_and_halo_specs(lat_rows.shape[0], d, lambda i: jnp.maximum(i - n_ctx_tiles, 0)))
        tok_args = (ctx_rows,) * 3 + (lat_rows,) * 3
    else:
        n, d = tokens.shape
        tok_specs = _tile_and_halo_specs(n, d, lambda i: i)
        tok_args = (tokens,) * 3
    n_tiles = n // t
    qk_w = 2 * HEADS * DQK
    v_w = HEADS * DV
    tile = lambda i: (i, 0)
    tok = np.arange(t)
    same_chunk = (tok[:, None] // CHUNK) == (tok[None, :] // CHUNK)
    tri = jnp.asarray(np.concatenate([same_chunk & (tok[None, :] <= tok[:, None]),
                                      same_chunk & (tok[None, :] >= tok[:, None])], axis=0), BF16)
    kern = functools.partial(_in_proj_kernel, n_ctx_tiles=n_ctx_tiles, n_tiles=n_tiles, two_sources=two_sources)
    return pl.pallas_call(
        kern,
        grid=(n_tiles,),
        in_specs=tok_specs + [
            _layer_spec(mod.shape[1:], layer), _layer_spec(g1.shape[1:], layer),
            _layer_spec((d, qk_w + v_w), layer), _layer_spec(wg.shape[1:], layer),
            _layer_spec(cw.shape[1:], layer), _layer_spec(cb.shape[1:], layer), _layer_spec(gb.shape[1:], layer),
            pl.BlockSpec(tri.shape, lambda i: (0, 0), pipeline_mode=pl.Buffered(1)),
        ],
        out_specs=[
            pl.BlockSpec((t, qk_w // 2), tile),
            pl.BlockSpec((t, qk_w // 2), tile),
            pl.BlockSpec((t, v_w), tile),
            pl.BlockSpec((t, 3 * LANES), tile),
            pl.BlockSpec((2 * HEADS, t), lambda i: (0, i)),
        ],
        out_shape=[
            jax.ShapeDtypeStruct((n, qk_w // 2), BF16),
            jax.ShapeDtypeStruct((n, qk_w // 2), BF16),
            jax.ShapeDtypeStruct((n, v_w), BF16),
            jax.ShapeDtypeStruct((n, 3 * LANES), F32),
            jax.ShapeDtypeStruct((2 * HEADS, n), F32),
        ],
        scratch_shapes=[pltpu.VMEM((t + 2 * HALO, COL_PIECE), F32)] * (qk_w // COL_PIECE),
        compiler_params=_params(),
        name="in_proj",
    )(*tok_args, mod, g1, w_qkvo, wg, cw, cb, gb, tri)


def _mlstm_kernel(qf_ref, kf_ref, vf_ref, colf_ref, rowf_ref, qb_ref, kb_ref, vb_ref, colb_ref, rowb_ref,
                  hf_ref, hb_ref, c_ref, n_ref, m_ref, p_ref, dl_ref, rs_ref, dn_ref):
    chunks = qf_ref.shape[0] // CHUNK

    @pl.when(pl.program_id(0) == 0)
    def _():
        c_ref[...] = jnp.zeros_like(c_ref)
        n_ref[...] = jnp.zeros_like(n_ref)
        m_ref[...] = jnp.zeros_like(m_ref)

    ti = lax.broadcasted_iota(jnp.int32, (CHUNK, CHUNK), 0)
    si = lax.broadcasted_iota(jnp.int32, (CHUNK, CHUNK), 1)
    lane_row = si[0:1, :]
    dirs = ((qf_ref, kf_ref, vf_ref, colf_ref, rowf_ref, hf_ref),
            (qb_ref, kb_ref, vb_ref, colb_ref, rowb_ref, hb_ref))

    heads = range(HEADS)
    qk_cols = lambda head: slice(head * DQK, (head + 1) * DQK)
    v_cols = lambda head: slice(head * DV, (head + 1) * DV)

    masks = [si <= ti, si >= ti]
    ends = [CHUNK - 1, 0]
    for c0 in range(0, chunks, STAGE1_CHUNKS):
        slots = [(c, r) for c in range(c0, c0 + STAGE1_CHUNKS) for r in range(2)]
        group = [(c, r, h) for c, r in slots for h in heads]
        rows = lambda c: slice(c * CHUNK, (c + 1) * CHUNK)
        amax_p = {(c, r): dirs[r][3][rows(c), LANES:2 * LANES] for c, r in slots}
        w_loc_p = {(c, r): jnp.exp(dirs[r][3][rows(c), 0:LANES] - amax_p[c, r][ends[r]:ends[r] + 1, :])
                   for c, r in slots}
        s_raw = [lax.dot_general(dirs[r][0][rows(c), qk_cols(h)], dirs[r][1][rows(c), qk_cols(h)], NT_DIMS,
                                 preferred_element_type=F32) for c, r, h in group]
        kw = [dirs[r][1][rows(c), qk_cols(h)].astype(F32) * w_loc_p[c, r][:, r * HEADS + h:r * HEADS + h + 1]
              for c, r, h in group]
        for j, (c, r, h) in enumerate(group):
            idx = (r * chunks + c) * HEADS + h
            dl_ref[idx] = lax.dot_general(kw[j].astype(BF16), dirs[r][2][rows(c), v_cols(h)], TN_DIMS,
                                          preferred_element_type=F32)
            dn_ref[idx:idx + 1, :] = jnp.sum(kw[j], axis=0, keepdims=True)
        s = []
        for j, (c, r, h) in enumerate(group):
            hd = r * HEADS + h
            decay = jnp.where(masks[r], jnp.exp(dirs[r][4][hd:hd + 1, rows(c)] - amax_p[c, r][:, hd:hd + 1]), 0.0)
            s.append(s_raw[j] * decay)
        for j, (c, r, h) in enumerate(group):
            idx = (r * chunks + c) * HEADS + h
            p_ref[idx] = _dot(s[j].astype(BF16), dirs[r][2][rows(c), v_cols(h)])
        for c, r in slots:
            rs_p = jnp.zeros((CHUNK, LANES), F32)
            for j, (cc, rr, h) in enumerate(group):
                if (cc, rr) == (c, r):
                    rs_p = jnp.where(si == r * HEADS + h, jnp.sum(s[j], axis=-1, keepdims=True), rs_p)
            rs_ref[r * chunks + c] = rs_p

    m_rows = [m_ref[0:1, :], m_ref[1:2, :]]
    for step in range(chunks):
        cs = [step, chunks - 1 - step]
        group = [(reverse, head) for reverse in range(2) for head in heads]
        rp, wp, sc_row, wc_row = [], [], [], []
        for reverse in range(2):
            q_ref, col_ref = dirs[reverse][0], dirs[reverse][3]
            end = 0 if reverse else CHUNK - 1
            in_dir = jnp.logical_and(lane_row >= reverse * HEADS, lane_row < (reverse + 1) * HEADS)
            rows = slice(cs[reverse] * CHUNK, (cs[reverse] + 1) * CHUNK)
            amax_p = col_ref[rows, LANES:2 * LANES]
            b_p = col_ref[rows, 2 * LANES:3 * LANES]
            m_row = m_rows[reverse]
            m_run = jnp.maximum(amax_p, m_row)
            r_p = jnp.exp(amax_p - m_run)
            w_prev_p = jnp.exp(m_row - m_run)
            qn_p = jnp.zeros((CHUNK, LANES), F32)
            for head in heads:
                hd = reverse * HEADS + head
                qn = jnp.sum(q_ref[rows, qk_cols(head)].astype(F32) * n_ref[hd:hd + 1, :], axis=-1, keepdims=True)
                qn_p = jnp.where(si == hd, qn, qn_p)
            den_p = r_p * rs_ref[reverse * chunks + cs[reverse]] + w_prev_p * qn_p
            inv_p = 1.0 / jnp.maximum(jnp.abs(den_p), jnp.exp(-(b_p + m_run)))
            rp.append(r_p * inv_p)
            wp.append(w_prev_p * inv_p)
            m_end = m_run[end:end + 1, :]
            sc_row.append(jnp.exp(amax_p[end:end + 1, :] - m_end))
            wc_row.append(jnp.exp(m_row - m_end))
            m_rows[reverse] = jnp.where(in_dir, b_p[end:end + 1, :] + m_end, m_row)
        c_state = [c_ref[r * HEADS + h] for r, h in group]
        inter = [_dot(dirs[r][0][slice(cs[r] * CHUNK, (cs[r] + 1) * CHUNK), qk_cols(h)], c_state[j].astype(BF16))
                 for j, (r, h) in enumerate(group)]
        for j, (r, h) in enumerate(group):
            hd = r * HEADS + h
            idx = (r * chunks + cs[r]) * HEADS + h
            wc = wc_row[r][:, hd:hd + 1]
            sc = sc_row[r][:, hd:hd + 1]
            c_ref[hd] = wc * c_state[j] + sc * dl_ref[idx]
            n_ref[hd:hd + 1, :] = wc * n_ref[hd:hd + 1, :] + sc * dn_ref[idx:idx + 1, :]
        for j, (r, h) in enumerate(group):
            hd = r * HEADS + h
            idx = (r * chunks + cs[r]) * HEADS + h
            rows = slice(cs[r] * CHUNK, (cs[r] + 1) * CHUNK)
            dirs[r][5][rows, v_cols(h)] = rp[r][:, hd:hd + 1] * p_ref[idx] + wp[r][:, hd:hd + 1] * inter[j]
    m_ref[0:1, :] = m_rows[0]
    m_ref[1:2, :] = m_rows[1]


def _mlstm_call(q, k, v, col, row, *, n_ctx_tiles):
    n = q.shape[0]
    t = TILE
    n_tiles = n // t
    slots = 2 * (t // CHUNK)
    fwd = lambda j: (j, 0)
    bwd_idx = lambda j: jnp.where(j < n_ctx_tiles, n_ctx_tiles - 1 - j, n_tiles - 1 + n_ctx_tiles - j)
    bwd = lambda j: (bwd_idx(j), 0)
    specs = lambda tok, rowm: [
        pl.BlockSpec((t, q.shape[1]), tok),
        pl.BlockSpec((t, k.shape[1]), tok),
        pl.BlockSpec((t, v.shape[1]), tok),
        pl.BlockSpec((t, col.shape[1]), tok),
        pl.BlockSpec((row.shape[0], t), rowm),
    ]
    return pl.pallas_call(
        _mlstm_kernel,
        grid=(n_tiles,),
        in_specs=specs(fwd, lambda j: (0, j)) + specs(bwd, lambda j: (0, bwd_idx(j))),
        out_specs=[pl.BlockSpec((t, v.shape[1]), fwd), pl.BlockSpec((t, v.shape[1]), bwd)],
        out_shape=[jax.ShapeDtypeStruct((n, v.shape[1]), F32)] * 2,
        scratch_shapes=[
            pltpu.VMEM((2 * HEADS, DQK, DV), F32),
            pltpu.VMEM((2 * HEADS, DQK), F32),
            pltpu.VMEM((8, LANES), F32),
            pltpu.VMEM((slots * HEADS, CHUNK, DV), F32),
            pltpu.VMEM((slots * HEADS, DQK, DV), F32),
            pltpu.VMEM((slots, CHUNK, LANES), F32),
            pltpu.VMEM((slots * HEADS, DQK), F32),
        ],
        compiler_params=_params(),
        name="mlstm",
    )(q, k, v, col, row, q, k, v, col, row)


def _mix_ffn_kernel(*refs, n_ctx_tiles, tile_off, final, two_sources):
    if two_sources:
        xc_ref, refs = refs[0], refs[1:]
    (x_ref, hf_ref, hb_ref, mod_ref, g_ref, wog_ref, wr_ref, hg_ref, vg_ref, wsp_ref, bsp_ref,
     wpa_ref, wpb_ref, wo_ref, g2_ref, w1_ref, w3_ref, w2_ref, fg_ref, o_ref, yb_ref) = refs
    d = x_ref.shape[1]
    t = x_ref.shape[0]
    gw = d // GROUPS
    i = pl.program_id(0) + tile_off
    is_ctx = i < n_ctx_tiles
    mod = jnp.where(is_ctx, mod_ref[1:2, :], mod_ref[0:1, :])
    gate1 = mod[:, 2 * d:3 * d]
    gate2 = mod[:, 5 * d:6 * d]
    x = jnp.where(is_ctx, xc_ref[...], x_ref[...]) if two_sources else x_ref[...]
    hb = _norm_mod(x, g_ref[...], mod, 0, d).astype(BF16)

    z_vb = _dot(hb, wr_ref[:, d:2 * d])
    z_u = _dot(hb, wr_ref[:, 0:d])
    vn = _rms(_gelu_tanh(z_vb), vg_ref[...]).astype(BF16)
    z_o = _dot(hb, wog_ref[...])
    u = _gelu_tanh(z_u)

    for c in range(t // CHUNK):
        rows = slice(c * CHUNK, (c + 1) * CHUNK)
        for grp in range(GROUPS):
            cols = slice(grp * gw, (grp + 1) * gw)
            sv = _dot(wsp_ref[grp], vn[rows, cols]) + bsp_ref[:, grp:grp + 1]
            yb_ref[rows, cols] = (u[rows, cols] * sv).astype(BF16)
    z_ga = _dot(hb, wr_ref[:, 2 * d:3 * d])

    o_gate = jax.nn.sigmoid(z_o)
    ya = []
    for head in range(HEADS):
        cols = slice(head * DV, (head + 1) * DV)
        h = hf_ref[:, cols] + hb_ref[:, cols]
        ya.append((_rms(h, hg_ref[:, cols]) * o_gate[:, cols]).astype(BF16))
    ya = jnp.concatenate(ya, axis=1)
    proj_b = _dot(yb_ref[...], wpb_ref[...])
    z_gb = _dot(hb, wr_ref[:, 3 * d:4 * d])
    proj_a = _dot(ya, wpa_ref[...])

    merged = jax.nn.sigmoid(z_ga) * proj_a + jax.nn.sigmoid(z_gb) * proj_b
    x = x + gate1 * _dot(merged.astype(BF16), wo_ref[...])

    hb = _norm_mod(x, g2_ref[...], mod, 1, d).astype(BF16)
    f = (_silu(_dot(hb, w1_ref[...])) * _dot(hb, w3_ref[...])).astype(BF16)
    y = x + gate2 * _dot(f, w2_ref[...])
    o_ref[...] = _rms(y, fg_ref[...]) if final else y


def _mix_ffn_call(tokens, hf, hb, mod, g1, w_qkvo, w_rest, hg, vg, wsp, bsp, wpa, wpb, wo, g2, w1, w3, w2, fg, *,
                  layer, n_ctx_tiles, tile_off, final):
    two_sources = isinstance(tokens, tuple)
    t = TILE
    d = hf.shape[1]
    n_out = hf.shape[0] - tile_off * t
    tin = lambda i: (i + tile_off, 0)
    if two_sources:
        last_ctx = tokens[0].shape[0] // t - 1
        tok_specs = [pl.BlockSpec((t, d), lambda i: (jnp.minimum(i + tile_off, last_ctx), 0)),
                     pl.BlockSpec((t, d), lambda i: (jnp.maximum(i + tile_off - n_ctx_tiles, 0), 0))]
    else:
        tokens = (tokens,)
        tok_specs = [pl.BlockSpec((t, d), tin)]
    o_block = (2 * HEADS * DQK + HEADS * DV) // d
    kern = functools.partial(_mix_ffn_kernel, n_ctx_tiles=n_ctx_tiles, tile_off=tile_off, final=final,
                             two_sources=two_sources)
    return pl.pallas_call(
        kern,
        grid=(n_out // t,),
        in_specs=tok_specs + [
            pl.BlockSpec((t, d), tin), pl.BlockSpec((t, d), tin),
            _layer_spec(mod.shape[1:], layer), _layer_spec(g1.shape[1:], layer),
            _layer_spec((d, d), layer, o_block), _layer_spec(w_rest.shape[1:], layer),
            _layer_spec(hg.shape[1:], layer), _layer_spec(vg.shape[1:], layer),
            pl.BlockSpec((None,) + wsp.shape[1:], lambda i: (layer, 0, 0, 0), pipeline_mode=pl.Buffered(1)),
            _layer_spec(bsp.shape[1:], layer),
            _layer_spec(wpa.shape[1:], layer), _layer_spec(wpb.shape[1:], layer), _layer_spec(wo.shape[1:], layer),
            _layer_spec(g2.shape[1:], layer),
            _layer_spec(w1.shape[1:], layer), _layer_spec(w3.shape[1:], layer), _layer_spec(w2.shape[1:], layer),
            pl.BlockSpec(fg.shape, lambda i: (0, 0)),
        ],
        out_specs=pl.BlockSpec((t, d), lambda i: (i, 0)),
        out_shape=jax.ShapeDtypeStruct((n_out, d), F32),
        scratch_shapes=[pltpu.VMEM((t, d), BF16)],
        compiler_params=_params(),
        name="mix_ffn",
    )(*tokens, hf, hb, mod, g1, w_qkvo, w_rest, hg, vg, wsp, bsp, wpa, wpb, wo, g2, w1, w3, w2, fg)


def _pad_lanes(a, width=LANES):
    return jnp.pad(a, ((0, 0),) * (a.ndim - 1) + ((0, width - a.shape[-1]),))


def kernel(x, c, ctx, c_ctx, w_ada, b_ada, norm1_g, norm2_g, w_in, conv_w, conv_b, gate_b, head_g, vnorm_g,
           w_spatial, b_spatial, w_pa, w_pb, w_out, w_ff1, w_ff3, w_ff2, final_g):
    batch, seq, d = x.shape
    ctx_len = ctx.shape[1]
    depth = w_ada.shape[0]
    assert batch == 1 and ctx_len % TILE == 0 and seq % TILE == 0
    n_ctx_tiles = ctx_len // TILE

    cc_t = jnp.zeros((d, 8), F32).at[:, 0].set(c[0]).at[:, 1].set(c_ctx)
    mod = _ada_call(cc_t, w_ada, b_ada)

    g0 = 2 * HEADS * DQK + 2 * HEADS * DV
    n_gate = 4 * HEADS
    w_qkvo, w_rest = _split_cast_call(w_in, g0, w_in.shape[2] - g0 - n_gate)
    w_gate = w_in[:, :, g0:g0 + n_gate].reshape(depth, d, 2, 2, HEADS)
    w_g = jnp.concatenate([_pad_lanes(w_gate[:, :, :, j, :].reshape(depth, d, 2 * HEADS)) for j in range(2)],
                          axis=2).astype(BF16)
    gb4 = gate_b.reshape(depth, 2, 2, HEADS)
    gb = jnp.concatenate([_pad_lanes(gb4[:, :, j, :].reshape(depth, 1, 2 * HEADS)) for j in range(2)], axis=2)
    cw = jnp.pad(conv_w, ((0, 0), (0, 8 - CONV_W), (0, 0)))
    bsp_t = _pad_lanes(jnp.swapaxes(b_spatial, 1, 2))
    wsp = w_spatial.astype(BF16)
    wpa, wpb, wo = w_pa.astype(BF16), w_pb.astype(BF16), w_out.astype(BF16)
    w1, w3, w2 = w_ff1.astype(BF16), w_ff3.astype(BF16), w_ff2.astype(BF16)
    rows = lambda a: a.reshape(depth, 1, -1)
    g1, g2, cb, hg, vg = rows(norm1_g), rows(norm2_g), rows(conv_b), rows(head_g), rows(vnorm_g)

    xs = (ctx[0], x[0])
    for l in range(depth):
        last = l == depth - 1
        off = n_ctx_tiles if last else 0
        q, k, v, col, rowp = _in_proj_call(xs, mod, g1, w_qkvo, w_g, cw, cb, gb,
                                           layer=l, n_ctx_tiles=n_ctx_tiles)
        hf, hb = _mlstm_call(q, k, v, col, rowp, n_ctx_tiles=n_ctx_tiles)
        xs = _mix_ffn_call(xs, hf, hb, mod, g1, w_qkvo, w_rest, hg, vg, wsp, bsp_t, wpa, wpb, wo,
                           g2, w1, w3, w2, final_g.reshape(1, -1),
                           layer=l, n_ctx_tiles=n_ctx_tiles, tile_off=off, final=last)
    return xs[None]
```

```python
import functools

import jax
import jax.numpy as jnp
import numpy as np
from jax import lax
from jax.experimental import pallas as pl
from jax.experimental.pallas import tpu as pltpu

F32 = jnp.float32
BF16 = jnp.bfloat16
HIGHEST = lax.Precision.HIGHEST

EPS = 1e-6
HEADS = 4
DQK = 128
DV = 256
GROUPS = 4
CHUNK = 128
CONV_W = 5
HALO = 8
TILE = 2 * CHUNK
LANES = 128
VMEM_LIMIT = 56 * 1024 * 1024
STAGE1_CHUNKS = 2

NT_DIMS = (((1,), (1,)), ((), ()))
TN_DIMS = (((0,), (0,)), ((), ()))


def _silu(x):
    return x * jax.nn.sigmoid(x)


def _gelu_tanh(x):
    return 0.5 * x * (1.0 + jnp.tanh(0.7978845608028654 * (x + 0.044715 * (x * x * x))))


def _rms(x, g):
    return x * lax.rsqrt(jnp.mean(x * x, axis=-1, keepdims=True) + EPS) * g


def _dot(a, b):
    return jnp.dot(a, b, preferred_element_type=F32)


def _layer_spec(shape, layer, col_block=0):
    return pl.BlockSpec((None,) + tuple(shape), lambda *_: (layer, 0, col_block), pipeline_mode=pl.Buffered(1))


def _params():
    return pltpu.CompilerParams(dimension_semantics=("arbitrary",), vmem_limit_bytes=VMEM_LIMIT)


def _ada_kernel(ct_ref, w_ref, b_ref, o_ref):
    s = _silu(ct_ref[...])
    w = w_ref[0]
    rows = [jnp.sum(w * s[:, r:r + 1], axis=0, keepdims=True) for r in range(2)]
    pad = jnp.zeros((o_ref.shape[1] - 2, w.shape[1]), F32)
    o_ref[0] = jnp.concatenate(rows + [pad], axis=0) + b_ref[0]


def _ada_call(cc_t, w_ada, b_ada):
    depth, d, width = w_ada.shape
    tn = width // 2
    return pl.pallas_call(
        _ada_kernel,
        grid=(depth, width // tn),
        in_specs=[
            pl.BlockSpec((d, 8), lambda l, j: (0, 0)),
            pl.BlockSpec((1, d, tn), lambda l, j: (l, 0, j)),
            pl.BlockSpec((1, 1, tn), lambda l, j: (l, 0, j)),
        ],
        out_specs=pl.BlockSpec((1, 8, tn), lambda l, j: (l, 0, j)),
        out_shape=jax.ShapeDtypeStruct((depth, 8, width), F32),
        compiler_params=pltpu.CompilerParams(
            dimension_semantics=("arbitrary", "arbitrary"), vmem_limit_bytes=VMEM_LIMIT),
        name="ada_mod",
    )(cc_t, w_ada, b_ada.reshape(depth, 1, width))


def _norm_mod(x, g, mod, which, d):
    shift = mod[:, (3 * which) * d:(3 * which + 1) * d]
    scale = mod[:, (3 * which + 1) * d:(3 * which + 2) * d]
    return _rms(x, g) * (1.0 + scale) + shift


def _in_proj_kernel(*refs, n_ctx_tiles, n_tiles, two_sources):
    n_tok = 6 if two_sources else 3
    tok_refs, refs = refs[:n_tok], refs[n_tok:]
    (mod_ref, g_ref, w_ref, wg_ref, cw_ref, cb_ref, gb_ref, tri_ref,
     qkv_ref, col_ref, row_ref, zs_ref) = refs
    t, d = tok_refs[0].shape
    q_w = HEADS * DQK
    qk_w = 2 * q_w
    q_ref, k_ref, v_ref = qkv_ref.at[:, 0:q_w], qkv_ref.at[:, q_w:qk_w], qkv_ref.at[:, qk_w:]
    i = pl.program_id(0)
    is_ctx = i < n_ctx_tiles
    mod = jnp.where(is_ctx, mod_ref[1:2, :], mod_ref[0:1, :])
    x_ext = jnp.concatenate([r[...] for r in tok_refs[-3:]], axis=0)
    if two_sources:
        x_ext = jnp.where(is_ctx, jnp.concatenate([r[...] for r in tok_refs[:3]], axis=0), x_ext)
    h_ext = _norm_mod(x_ext, g_ref[...], mod, 0, d).astype(BF16)
    hb = h_ext[0:t]

    gates = _dot(hb, wg_ref[...]) + gb_ref[...]
    gi = gates[:, 0:LANES]
    gf = gates[:, LANES:2 * LANES]
    logf = jnp.minimum(gf, 0.0) - jnp.log1p(jnp.exp(-jnp.abs(gf)))

    tri = tri_ref[...]
    p0 = logf.astype(BF16)
    r1 = logf - p0.astype(F32)
    p1 = r1.astype(BF16)
    p2 = (r1 - p1.astype(F32)).astype(BF16)
    s01 = _dot(tri, jnp.concatenate([p0, p1], axis=1))
    sums = s01[:, 0:LANES] + s01[:, LANES:2 * LANES] + _dot(tri, p2)

    prev_ok = jnp.logical_and(i != 0, i != n_ctx_tiles)
    next_ok = jnp.logical_and(i != n_ctx_tiles - 1, i != n_tiles - 1)
    half = CONV_W // 2
    n_groups = t // HALO
    sub = lax.broadcasted_iota(jnp.int32, (HALO, qk_w), 0)

    z_ext = _dot(h_ext, w_ref[:, 0:qk_w])
    zs_ref[0:HALO, :] = jnp.where(prev_ok, z_ext[t:t + HALO], 0.0)
    zs_ref[HALO:HALO + t, :] = z_ext[0:t]
    zs_ref[HALO + t:2 * HALO + t, :] = jnp.where(next_ok, z_ext[t + HALO:t + 2 * HALO], 0.0)
    v_ref[...] = _dot(hb, w_ref[:, qk_w:qk_w + v_ref.shape[1]]).astype(BF16)

    groups = [zs_ref[m * HALO:(m + 1) * HALO, :] for m in range(n_groups + 2)]
    rot = {o: [pltpu.roll(gm, (HALO - o) % HALO, axis=0) for gm in groups]
           for o in range(-half, half + 1) if o != 0}
    accs = []
    for gidx in range(n_groups):
        acc = cb_ref[...] + cw_ref[half:half + 1, :] * groups[gidx + 1]
        for o in range(1, half + 1):
            ahead = jnp.where(sub < HALO - o, rot[o][gidx + 1], rot[o][gidx + 2])
            behind = jnp.where(sub >= o, rot[-o][gidx + 1], rot[-o][gidx])
            acc = acc + cw_ref[half + o:half + o + 1, :] * ahead + cw_ref[half - o:half - o + 1, :] * behind
        accs.append(acc)
    qk = _silu(jnp.concatenate(accs, axis=0))
    q_ref[...] = (qk[:, 0:q_w] * (DQK ** -0.5)).astype(BF16)
    k_ref[...] = qk[:, q_w:].astype(BF16)

    ti = lax.broadcasted_iota(jnp.int32, (CHUNK, CHUNK), 0)
    si = lax.broadcasted_iota(jnp.int32, (CHUNK, CHUNK), 1)
    lower = si <= ti
    upper = si >= ti
    used = si < 2 * HEADS
    for c in range(t // CHUNK):
        rows = slice(c * CHUNK, (c + 1) * CHUNK)
        prefix = sums[c * CHUNK:(c + 1) * CHUNK]
        suffix = sums[t + c * CHUNK:t + (c + 1) * CHUNK]
        b = jnp.where(used, jnp.where(si < HEADS, prefix, suffix), 0.0)
        a = jnp.where(used, gi[rows] - b, 0.0)
        a_t = a.T
        a_max = jnp.zeros((CHUNK, LANES), F32)
        for hd in range(2 * HEADS):
            msk = lower if hd < HEADS else upper
            run = jnp.max(jnp.where(msk, a_t[hd:hd + 1, :], -jnp.inf), axis=-1, keepdims=True)
            a_max = jnp.where(si == hd, run, a_max)
        col_ref[rows, 0:LANES] = a
        col_ref[rows, LANES:2 * LANES] = a_max
        col_ref[rows, 2 * LANES:3 * LANES] = b
        row_ref[:, rows] = a_t[0:2 * HEADS, :]


def _tile_and_halo_specs(rows, d, tile_of):
    per_tile = TILE // HALO
    last_halo = rows // HALO - 1
    return [
        pl.BlockSpec((TILE, d), lambda i: (tile_of(i), 0)),
        pl.BlockSpec((HALO, d), lambda i: (jnp.maximum(tile_of(i) * per_tile - 1, 0), 0)),
        pl.BlockSpec((HALO, d), lambda i: (jnp.minimum((tile_of(i) + 1) * per_tile, last_halo), 0)),
    ]


def _in_proj_call(tokens, mod, g1, w_qkvo, wg, cw, cb, gb, *, layer, n_ctx_tiles):
    two_sources = isinstance(tokens, tuple)
    t = TILE
    if two_sources:
        ctx_rows, lat_rows = tokens
        d = lat_rows.shape[1]
        n = ctx_rows.shape[0] + lat_rows.shape[0]
        tok_specs = (_tile_and_halo_specs(ctx_rows.shape[0], d, lambda i: jnp.minimum(i, n_ctx_tiles - 1))
                     + _tile_and_halo_specs(lat_rows.shape[0], d, lambda i: jnp.maximum(i - n_ctx_tiles, 0)))
        tok_args = (ctx_rows,) * 3 + (lat_rows,) * 3
    else:
        n, d = tokens.shape
        tok_specs = _tile_and_halo_specs(n, d, lambda i: i)
        tok_args = (tokens,) * 3
    n_tiles = n // t
    qk_w = 2 * HEADS * DQK
    v_w = HEADS * DV
    tile = lambda i: (i, 0)
    tok = np.arange(t)
    same_chunk = (tok[:, None] // CHUNK) == (tok[None, :] // CHUNK)
    tri = jnp.asarray(np.concatenate([same_chunk & (tok[None, :] <= tok[:, None]),
                                      same_chunk & (tok[None, :] >= tok[:, None])], axis=0), BF16)
    kern = functools.partial(_in_proj_kernel, n_ctx_tiles=n_ctx_tiles, n_tiles=n_tiles, two_sources=two_sources)
    return pl.pallas_call(
        kern,
        grid=(n_tiles,),
        in_specs=tok_specs + [
            _layer_spec(mod.shape[1:], layer), _layer_spec(g1.shape[1:], layer),
            _layer_spec((d, qk_w + v_w), layer), _layer_spec(wg.shape[1:], layer),
            _layer_spec(cw.shape[1:], layer), _layer_spec(cb.shape[1:], layer), _layer_spec(gb.shape[1:], layer),
            pl.BlockSpec(tri.shape, lambda i: (0, 0), pipeline_mode=pl.Buffered(1)),
        ],
        out_specs=[
            pl.BlockSpec((t, qk_w + v_w), tile),
            pl.BlockSpec((t, 3 * LANES), tile),
            pl.BlockSpec((2 * HEADS, t), lambda i: (0, i)),
        ],
        out_shape=[
            jax.ShapeDtypeStruct((n, qk_w + v_w), BF16),
            jax.ShapeDtypeStruct((n, 3 * LANES), F32),
            jax.ShapeDtypeStruct((2 * HEADS, n), F32),
        ],
        scratch_shapes=[pltpu.VMEM((t + 2 * HALO, qk_w), F32)],
        compiler_params=_params(),
        name="in_proj",
    )(*tok_args, mod, g1, w_qkvo, wg, cw, cb, gb, tri)


def _mlstm_kernel(qkvf_ref, colf_ref, rowf_ref, qkvb_ref, colb_ref, rowb_ref,
                  hf_ref, hb_ref, c_ref, n_ref, m_ref, p_ref, dl_ref, rs_ref, dn_ref):
    chunks = qkvf_ref.shape[0] // CHUNK
    q_w = HEADS * DQK
    (qf_ref, kf_ref, vf_ref), (qb_ref, kb_ref, vb_ref) = (
        (r.at[:, 0:q_w], r.at[:, q_w:2 * q_w], r.at[:, 2 * q_w:]) for r in (qkvf_ref, qkvb_ref))

    @pl.when(pl.program_id(0) == 0)
    def _():
        c_ref[...] = jnp.zeros_like(c_ref)
        n_ref[...] = jnp.zeros_like(n_ref)
        m_ref[...] = jnp.zeros_like(m_ref)

    ti = lax.broadcasted_iota(jnp.int32, (CHUNK, CHUNK), 0)
    si = lax.broadcasted_iota(jnp.int32, (CHUNK, CHUNK), 1)
    lane_row = si[0:1, :]
    dirs = ((qf_ref, kf_ref, vf_ref, colf_ref, rowf_ref, hf_ref),
            (qb_ref, kb_ref, vb_ref, colb_ref, rowb_ref, hb_ref))

    heads = range(HEADS)
    qk_cols = lambda head: slice(head * DQK, (head + 1) * DQK)
    v_cols = lambda head: slice(head * DV, (head + 1) * DV)

    masks = [si <= ti, si >= ti]
    ends = [CHUNK - 1, 0]
    for c0 in range(0, chunks, STAGE1_CHUNKS):
        slots = [(c, r) for c in range(c0, c0 + STAGE1_CHUNKS) for r in range(2)]
        group = [(c, r, h) for c, r in slots for h in heads]
        rows = lambda c: slice(c * CHUNK, (c + 1) * CHUNK)
        amax_p = {(c, r): dirs[r][3][rows(c), LANES:2 * LANES] for c, r in slots}
        w_loc_p = {(c, r): jnp.exp(dirs[r][3][rows(c), 0:LANES] - amax_p[c, r][ends[r]:ends[r] + 1, :])
                   for c, r in slots}
        s_raw = [lax.dot_general(dirs[r][0][rows(c), qk_cols(h)], dirs[r][1][rows(c), qk_cols(h)], NT_DIMS,
                                 preferred_element_type=F32) for c, r, h in group]
        kw = [dirs[r][1][rows(c), qk_cols(h)].astype(F32) * w_loc_p[c, r][:, r * HEADS + h:r * HEADS + h + 1]
              for c, r, h in group]
        for j, (c, r, h) in enumerate(group):
            idx = (r * chunks + c) * HEADS + h
            dl_ref[idx] = lax.dot_general(kw[j].astype(BF16), dirs[r][2][rows(c), v_cols(h)], TN_DIMS,
                                          preferred_element_type=F32)
            dn_ref[idx:idx + 1, :] = jnp.sum(kw[j], axis=0, keepdims=True)
        s = []
        for j, (c, r, h) in enumerate(group):
            hd = r * HEADS + h
            decay = jnp.where(masks[r], jnp.exp(dirs[r][4][hd:hd + 1, rows(c)] - amax_p[c, r][:, hd:hd + 1]), 0.0)
            s.append(s_raw[j] * decay)
        for j, (c, r, h) in enumerate(group):
            idx = (r * chunks + c) * HEADS + h
            p_ref[idx] = _dot(s[j].astype(BF16), dirs[r][2][rows(c), v_cols(h)])
        for c, r in slots:
            rs_p = jnp.zeros((CHUNK, LANES), F32)
            for j, (cc, rr, h) in enumerate(group):
                if (cc, rr) == (c, r):
                    rs_p = jnp.where(si == r * HEADS + h, jnp.sum(s[j], axis=-1, keepdims=True), rs_p)
            rs_ref[r * chunks + c] = rs_p

    m_rows = [m_ref[0:1, :], m_ref[1:2, :]]
    for step in range(chunks):
        cs = [step, chunks - 1 - step]
        group = [(reverse, head) for reverse in range(2) for head in heads]
        rp, wp, sc_row, wc_row = [], [], [], []
        for reverse in range(2):
            q_ref, col_ref = dirs[reverse][0], dirs[reverse][3]
            end = 0 if reverse else CHUNK - 1
            in_dir = jnp.logical_and(lane_row >= reverse * HEADS, lane_row < (reverse + 1) * HEADS)
            rows = slice(cs[reverse] * CHUNK, (cs[reverse] + 1) * CHUNK)
            amax_p = col_ref[rows, LANES:2 * LANES]
            b_p = col_ref[rows, 2 * LANES:3 * LANES]
            m_row = m_rows[reverse]
            m_run = jnp.maximum(amax_p, m_row)
            r_p = jnp.exp(amax_p - m_run)
            w_prev_p = jnp.exp(m_row - m_run)
            qn_p = jnp.zeros((CHUNK, LANES), F32)
            for head in heads:
                hd = reverse * HEADS + head
                qn = jnp.sum(q_ref[rows, qk_cols(head)].astype(F32) * n_ref[hd:hd + 1, :], axis=-1, keepdims=True)
                qn_p = jnp.where(si == hd, qn, qn_p)
            den_p = r_p * rs_ref[reverse * chunks + cs[reverse]] + w_prev_p * qn_p
            inv_p = 1.0 / jnp.maximum(jnp.abs(den_p), jnp.exp(-(b_p + m_run)))
            rp.append(r_p * inv_p)
            wp.append(w_prev_p * inv_p)
            m_end = m_run[end:end + 1, :]
            sc_row.append(jnp.exp(amax_p[end:end + 1, :] - m_end))
            wc_row.append(jnp.exp(m_row - m_end))
            m_rows[reverse] = jnp.where(in_dir, b_p[end:end + 1, :] + m_end, m_row)
        c_state = [c_ref[r * HEADS + h] for r, h in group]
        inter = [_dot(dirs[r][0][slice(cs[r] * CHUNK, (cs[r] + 1) * CHUNK), qk_cols(h)], c_state[j].astype(BF16))
                 for j, (r, h) in enumerate(group)]
        for j, (r, h) in enumerate(group):
            hd = r * HEADS + h
            idx = (r * chunks + cs[r]) * HEADS + h
            wc = wc_row[r][:, hd:hd + 1]
            sc = sc_row[r][:, hd:hd + 1]
            c_ref[hd] = wc * c_state[j] + sc * dl_ref[idx]
            n_ref[hd:hd + 1, :] = wc * n_ref[hd:hd + 1, :] + sc * dn_ref[idx:idx + 1, :]
        for j, (r, h) in enumerate(group):
            hd = r * HEADS + h
            idx = (r * chunks + cs[r]) * HEADS + h
            rows = slice(cs[r] * CHUNK, (cs[r] + 1) * CHUNK)
            dirs[r][5][rows, v_cols(h)] = rp[r][:, hd:hd + 1] * p_ref[idx] + wp[r][:, hd:hd + 1] * inter[j]
    m_ref[0:1, :] = m_rows[0]
    m_ref[1:2, :] = m_rows[1]


def _mlstm_call(qkv, col, row, *, n_ctx_tiles):
    n = qkv.shape[0]
    v_w = HEADS * DV
    t = TILE
    n_tiles = n // t
    slots = 2 * (t // CHUNK)
    fwd = lambda j: (j, 0)
    bwd_idx = lambda j: jnp.where(j < n_ctx_tiles, n_ctx_tiles - 1 - j, n_tiles - 1 + n_ctx_tiles - j)
    bwd = lambda j: (bwd_idx(j), 0)
    specs = lambda tok, rowm: [
        pl.BlockSpec((t, qkv.shape[1]), tok),
        pl.BlockSpec((t, col.shape[1]), tok),
        pl.BlockSpec((row.shape[0], t), rowm),
    ]
    return pl.pallas_call(
        _mlstm_kernel,
        grid=(n_tiles,),
        in_specs=specs(fwd, lambda j: (0, j)) + specs(bwd, lambda j: (0, bwd_idx(j))),
        out_specs=[pl.BlockSpec((t, v_w), fwd), pl.BlockSpec((t, v_w), bwd)],
        out_shape=[jax.ShapeDtypeStruct((n, v_w), F32)] * 2,
        scratch_shapes=[
            pltpu.VMEM((2 * HEADS, DQK, DV), F32),
            pltpu.VMEM((2 * HEADS, DQK), F32),
            pltpu.VMEM((8, LANES), F32),
            pltpu.VMEM((slots * HEADS, CHUNK, DV), F32),
            pltpu.VMEM((slots * HEADS, DQK, DV), F32),
            pltpu.VMEM((slots, CHUNK, LANES), F32),
            pltpu.VMEM((slots * HEADS, DQK), F32),
        ],
        compiler_params=_params(),
        name="mlstm",
    )(qkv, col, row, qkv, col, row)


def _mix_ffn_kernel(*refs, n_ctx_tiles, tile_off, final, two_sources):
    if two_sources:
        xc_ref, refs = refs[0], refs[1:]
    (x_ref, hf_ref, hb_ref, mod_ref, g_ref, wog_ref, wr_ref, hg_ref, vg_ref, wsp_ref, bsp_ref,
     wpa_ref, wpb_ref, wo_ref, g2_ref, w1_ref, w3_ref, w2_ref, fg_ref, o_ref, yb_ref) = refs
    d = x_ref.shape[1]
    t = x_ref.shape[0]
    gw = d // GROUPS
    i = pl.program_id(0) + tile_off
    is_ctx = i < n_ctx_tiles
    mod = jnp.where(is_ctx, mod_ref[1:2, :], mod_ref[0:1, :])
    gate1 = mod[:, 2 * d:3 * d]
    gate2 = mod[:, 5 * d:6 * d]
    x = jnp.where(is_ctx, xc_ref[...], x_ref[...]) if two_sources else x_ref[...]
    hb = _norm_mod(x, g_ref[...], mod, 0, d).astype(BF16)

    z_vb = _dot(hb, wr_ref[:, d:2 * d])
    z_u = _dot(hb, wr_ref[:, 0:d])
    vn = _rms(_gelu_tanh(z_vb), vg_ref[...]).astype(BF16)
    z_o = _dot(hb, wog_ref[...])
    u = _gelu_tanh(z_u)

    for c in range(t // CHUNK):
        rows = slice(c * CHUNK, (c + 1) * CHUNK)
        for grp in range(GROUPS):
            cols = slice(grp * gw, (grp + 1) * gw)
            sv = _dot(wsp_ref[grp], vn[rows, cols]) + bsp_ref[:, grp:grp + 1]
            yb_ref[rows, cols] = (u[rows, cols] * sv).astype(BF16)
    z_ga = _dot(hb, wr_ref[:, 2 * d:3 * d])

    o_gate = jax.nn.sigmoid(z_o)
    ya = []
    for head in range(HEADS):
        cols = slice(head * DV, (head + 1) * DV)
        h = hf_ref[:, cols] + hb_ref[:, cols]
        ya.append((_rms(h, hg_ref[:, cols]) * o_gate[:, cols]).astype(BF16))
    ya = jnp.concatenate(ya, axis=1)
    proj_b = _dot(yb_ref[...], wpb_ref[...])
    z_gb = _dot(hb, wr_ref[:, 3 * d:4 * d])
    proj_a = _dot(ya, wpa_ref[...])

    merged = jax.nn.sigmoid(z_ga) * proj_a + jax.nn.sigmoid(z_gb) * proj_b
    x = x + gate1 * _dot(merged.astype(BF16), wo_ref[...])

    hb = _norm_mod(x, g2_ref[...], mod, 1, d).astype(BF16)
    f = (_silu(_dot(hb, w1_ref[...])) * _dot(hb, w3_ref[...])).astype(BF16)
    y = x + gate2 * _dot(f, w2_ref[...])
    o_ref[...] = _rms(y, fg_ref[...]) if final else y


def _mix_ffn_call(tokens, hf, hb, mod, g1, w_qkvo, w_rest, hg, vg, wsp, bsp, wpa, wpb, wo, g2, w1, w3, w2, fg, *,
                  layer, n_ctx_tiles, tile_off, final):
    two_sources = isinstance(tokens, tuple)
    t = TILE
    d = hf.shape[1]
    n_out = hf.shape[0] - tile_off * t
    tin = lambda i: (i + tile_off, 0)
    if two_sources:
        last_ctx = tokens[0].shape[0] // t - 1
        tok_specs = [pl.BlockSpec((t, d), lambda i: (jnp.minimum(i + tile_off, last_ctx), 0)),
                     pl.BlockSpec((t, d), lambda i: (jnp.maximum(i + tile_off - n_ctx_tiles, 0), 0))]
    else:
        tokens = (tokens,)
        tok_specs = [pl.BlockSpec((t, d), tin)]
    o_block = (2 * HEADS * DQK + HEADS * DV) // d
    kern = functools.partial(_mix_ffn_kernel, n_ctx_tiles=n_ctx_tiles, tile_off=tile_off, final=final,
                             two_sources=two_sources)
    return pl.pallas_call(
        kern,
        grid=(n_out // t,),
        in_specs=tok_specs + [
            pl.BlockSpec((t, d), tin), pl.BlockSpec((t, d), tin),
            _layer_spec(mod.shape[1:], layer), _layer_spec(g1.shape[1:], layer),
            _layer_spec((d, d), layer, o_block), _layer_spec(w_rest.shape[1:], layer),
            _layer_spec(hg.shape[1:], layer), _layer_spec(vg.shape[1:], layer),
            pl.BlockSpec((None,) + wsp.shape[1:], lambda i: (layer, 0, 0, 0), pipeline_mode=pl.Buffered(1)),
            _layer_spec(bsp.shape[1:], layer),
            _layer_spec(wpa.shape[1:], layer), _layer_spec(wpb.shape[1:], layer), _layer_spec(wo.shape[1:], layer),
            _layer_spec(g2.shape[1:], layer),
            _layer_spec(w1.shape[1:], layer), _layer_spec(w3.shape[1:], layer), _layer_spec(w2.shape[1:], layer),
            pl.BlockSpec(fg.shape, lambda i: (0, 0)),
        ],
        out_specs=pl.BlockSpec((t, d), lambda i: (i, 0)),
        out_shape=jax.ShapeDtypeStruct((n_out, d), F32),
        scratch_shapes=[pltpu.VMEM((t, d), BF16)],
        compiler_params=_params(),
        name="mix_ffn",
    )(*tokens, hf, hb, mod, g1, w_qkvo, w_rest, hg, vg, wsp, bsp, wpa, wpb, wo, g2, w1, w3, w2, fg)


def _pad_lanes(a, width=LANES):
    return jnp.pad(a, ((0, 0),) * (a.ndim - 1) + ((0, width - a.shape[-1]),))


def kernel(x, c, ctx, c_ctx, w_ada, b_ada, norm1_g, norm2_g, w_in, conv_w, conv_b, gate_b, head_g, vnorm_g,
           w_spatial, b_spatial, w_pa, w_pb, w_out, w_ff1, w_ff3, w_ff2, final_g):
    batch, seq, d = x.shape
    ctx_len = ctx.shape[1]
    depth = w_ada.shape[0]
    assert batch == 1 and ctx_len % TILE == 0 and seq % TILE == 0
    n_ctx_tiles = ctx_len // TILE

    cc_t = jnp.zeros((d, 8), F32).at[:, 0].set(c[0]).at[:, 1].set(c_ctx)
    mod = _ada_call(cc_t, w_ada, b_ada)

    g0 = 2 * HEADS * DQK + 2 * HEADS * DV
    n_gate = 4 * HEADS
    w_qkvo = w_in.astype(BF16)
    w_rest = w_qkvo[:, :, g0 + n_gate:]
    w_gate = w_in[:, :, g0:g0 + n_gate].reshape(depth, d, 2, 2, HEADS)
    w_g = jnp.concatenate([_pad_lanes(w_gate[:, :, :, j, :].reshape(depth, d, 2 * HEADS)) for j in range(2)],
                          axis=2).astype(BF16)
    gb4 = gate_b.reshape(depth, 2, 2, HEADS)
    gb = jnp.concatenate([_pad_lanes(gb4[:, :, j, :].reshape(depth, 1, 2 * HEADS)) for j in range(2)], axis=2)
    cw = jnp.pad(conv_w, ((0, 0), (0, 8 - CONV_W), (0, 0)))
    bsp_t = _pad_lanes(jnp.swapaxes(b_spatial, 1, 2))
    wsp = w_spatial.astype(BF16)
    wpa, wpb, wo = w_pa.astype(BF16), w_pb.astype(BF16), w_out.astype(BF16)
    w1, w3, w2 = w_ff1.astype(BF16), w_ff3.astype(BF16), w_ff2.astype(BF16)
    rows = lambda a: a.reshape(depth, 1, -1)
    g1, g2, cb, hg, vg = rows(norm1_g), rows(norm2_g), rows(conv_b), rows(head_g), rows(vnorm_g)

    xs = (ctx[0], x[0])
    for l in range(depth):
        last = l == depth - 1
        off = n_ctx_tiles if last else 0
        qkv, col, rowp = _in_proj_call(xs, mod, g1, w_qkvo, w_g, cw, cb, gb, layer=l, n_ctx_tiles=n_ctx_tiles)
        hf, hb = _mlstm_call(qkv, col, rowp, n_ctx_tiles=n_ctx_tiles)
        xs = _mix_ffn_call(xs, hf, hb, mod, g1, w_qkvo, w_rest, hg, vg, wsp, bsp_t, wpa, wpb, wo,
                           g2, w1, w3, w2, final_g.reshape(1, -1),
                           layer=l, n_ctx_tiles=n_ctx_tiles, tile_off=off, final=last)
    return xs[None]
```
